```python
import jax, jax.numpy as jnp
from jax import lax
import numpy as np

D_MODEL = 1024
BATCH = 32
SEQ = 256
DEPTH = 2
DEC_BATCH = 4
DEC_SEQ = 2048
PAST_LEN = 512

GRID_W = 64
HEAD_DIM = 64
A_HEADS = 6
A_KV_HEADS = 2
A_GROUP = A_HEADS // A_KV_HEADS
B_HEADS = 4
C_HEADS = 6
A_Q_W = A_HEADS * HEAD_DIM
A_KV_W = A_KV_HEADS * HEAD_DIM
B_W = B_HEADS * HEAD_DIM
C_W = C_HEADS * HEAD_DIM
MIX_WIDTH = A_Q_W + B_W + C_W
IN_SIZES = (A_Q_W, A_KV_W, A_KV_W, B_W, B_W, B_W, B_W, 4 * B_HEADS, C_W, C_W, C_W)
P_IN = sum(IN_SIZES)
Q_BLOCK = 128
MLSTM_CHUNK = 64
MLSTM_F_BIAS = 3.0
NA_WIN_R = 8
NA_WIN_C = 16
ROPE_THETA = 10000.0
ROPE_QUARTER = HEAD_DIM // 4
ATTN_SCALE = HEAD_DIM ** -0.5
D_FF = 2816
N_EXPERTS = 8
TOP_K = 2
D_FF_EXPERT = 3584
N_DENSE = (DEPTH + 1) // 2
N_MOE = DEPTH // 2
EPS = 1e-6

kernel_name = "hybrid_diffusion_gqa_mlstm_natten_step"


def rmsnorm(x, g):
    xf = x.astype(jnp.float32)
    y = xf * lax.rsqrt(jnp.mean(xf * xf, axis=-1, keepdims=True) + EPS)
    return (y * g.astype(jnp.float32)).astype(x.dtype)


def _split_points():
    pts, s = [], 0
    for n in IN_SIZES[:-1]:
        s += n
        pts.append(s)
    return pts


def axial_rope_tables(n):
    t = jnp.arange(n, dtype=jnp.int32)
    pos = jnp.stack([t // GRID_W, t % GRID_W], axis=-1).astype(jnp.float32)
    inv = ROPE_THETA ** (-jnp.arange(ROPE_QUARTER, dtype=jnp.float32) / ROPE_QUARTER)
    ang = pos[:, :, None] * inv
    return jnp.cos(ang), jnp.sin(ang)


def apply_axial_rope(x, cos, sin):
    B, N, H, D = x.shape
    xf = x.astype(jnp.float32).reshape(B, N, H, 2, 2, D // 4)
    x1, x2 = xf[..., 0, :], xf[..., 1, :]
    c, s = cos[None, :, None], sin[None, :, None]
    out = jnp.stack([x1 * c - x2 * s, x2 * c + x1 * s], axis=-2).reshape(B, N, H, D)
    return out.astype(x.dtype)


def block_attention(q, k, v):
    B, Nq, KV, G, D = q.shape
    nb = Nq // Q_BLOCK
    qb = jnp.moveaxis(q.reshape(B, nb, Q_BLOCK, KV, G, D), 1, 0)

    def one_block(q_blk):
        s = jnp.einsum('bqkgd,bskd->bkgqs', q_blk, k, preferred_element_type=jnp.float32)
        p = jax.nn.softmax(s, axis=-1).astype(v.dtype)
        return jnp.einsum('bkgqs,bskd->bqkgd', p, v)

    out = lax.map(one_block, qb)
    return jnp.moveaxis(out, 0, 1).reshape(B, Nq, KV * G * D)


def neighbourhood_attention(q, k, v, k_ctx, v_ctx, rpb):
    B, N, H, D = q.shape
    rows = N // GRID_W
    wr = min(NA_WIN_R, rows)
    wc = NA_WIN_C
    n_nb = wr * wc
    qg = q.reshape(B, rows, GRID_W, H, D)
    kg = k.reshape(B, rows, GRID_W, H, D)
    vg = v.reshape(B, rows, GRID_W, H, D)
    k_ctx = k_ctx.astype(q.dtype)
    v_ctx = v_ctx.astype(v.dtype)
    cols = jnp.arange(GRID_W, dtype=jnp.int32)
    c_start = jnp.clip(cols - wc // 2, 0, GRID_W - wc)
    col_idx = c_start[:, None] + jnp.arange(wc, dtype=jnp.int32)[None, :]
    col_off = col_idx - cols[:, None] + (NA_WIN_C - 1)
    rpb_c = rpb.astype(jnp.float32)[:, :, col_off]

    def one_row(r):
        r_start = jnp.clip(r - wr // 2, 0, rows - wr)
        q_r = lax.dynamic_index_in_dim(qg, r, axis=1, keepdims=False)
        k_nb = lax.dynamic_slice_in_dim(kg, r_start, wr, axis=1)[:, :, col_idx]
        v_nb = lax.dynamic_slice_in_dim(vg, r_start, wr, axis=1)[:, :, col_idx]
        row_off = r_start + jnp.arange(wr, dtype=jnp.int32) - r + (NA_WIN_R - 1)
        bias = jnp.take(rpb_c, row_off, axis=1).transpose(0, 2, 1, 3)
        s_nb = jnp.einsum('bqhd,bwqjhd->bhqwj', q_r, k_nb, preferred_element_type=jnp.float32) + bias[None]
        s_ctx = jnp.einsum('bqhd,bshd->bhqs', q_r, k_ctx, preferred_element_type=jnp.float32)
        s = jnp.concatenate([s_nb.reshape(B, H, GRID_W, n_nb), s_ctx], axis=-1)
        p = jax.nn.softmax(s, axis=-1).astype(v.dtype)
        p_nb = p[..., :n_nb].reshape(B, H, GRID_W, wr, wc)
        return (jnp.einsum('bhqwj,bwqjhd->bqhd', p_nb, v_nb)
                + jnp.einsum('bhqs,bshd->bqhd', p[..., n_nb:], v_ctx))

    out = lax.map(one_row, jnp.arange(rows, dtype=jnp.int32))
    return jnp.moveaxis(out, 0, 1).reshape(B, N, H * D)


def mlstm_chunkwise(q, k, v, log_i, log_f, C0, n0, m0):
    B, H, N, D = q.shape
    L = MLSTM_CHUNK
    nc = N // L
    to_chunks = lambda t: jnp.moveaxis(t.astype(jnp.float32).reshape(B, H, nc, L, *t.shape[3:]), 2, 0)
    xs = (to_chunks(q), to_chunks(k), to_chunks(v), to_chunks(log_i), to_chunks(log_f))
    tril = jnp.tril(jnp.ones((L, L), dtype=bool))

    def step(carry, xs_c):
        C, n, m = carry
        qc, kc, vc, li, lf = xs_c
        b = jnp.cumsum(lf, axis=-1)
        logD = jnp.where(tril, b[..., :, None] - b[..., None, :] + li[..., None, :], -jnp.inf)
        m_inter = b + m[..., None]
        m_t = jnp.maximum(m_inter, jnp.max(logD, axis=-1))
        w_inter = jnp.exp(m_inter - m_t)
        s = jnp.einsum('bhtd,bhsd->bhts', qc, kc) * jnp.exp(logD - m_t[..., None])
        num = jnp.einsum('bhts,bhsd->bhtd', s, vc) + w_inter[..., None] * jnp.einsum('bhtd,bhde->bhte', qc, C)
        den = jnp.sum(s, axis=-1) + w_inter * jnp.einsum('bhtd,bhd->bht', qc, n)
        h = num / jnp.maximum(jnp.abs(den), jnp.exp(-m_t))[..., None]
        m_new = m_t[..., -1]
        w = jnp.exp(b[..., -1:] - b + li - m_new[..., None])
        decay = jnp.exp(b[..., -1] + m - m_new)
        C_new = decay[..., None, None] * C + jnp.einsum('bhs,bhsd,bhse->bhde', w, kc, vc)
        n_new = decay[..., None] * n + jnp.einsum('bhs,bhsd->bhd', w, kc)
        return (C_new, n_new, m_new), h

    init = (C0.astype(jnp.float32), n0.astype(jnp.float32), m0.astype(jnp.float32))
    (C, n, m), hs = lax.scan(step, init, xs)
    return jnp.moveaxis(hs, 0, 2).reshape(B, H, N, D), C, n, m


def mlstm_mixer(q, k, v, o, g, gate_bias, g_out, C0, n0, m0):
    B, N, H, D = q.shape
    qh = jnp.swapaxes(q * ATTN_SCALE, 1, 2)
    kh = jnp.swapaxes(k, 1, 2)
    vh = jnp.swapaxes(v, 1, 2)
    pre = g.astype(jnp.float32) + gate_bias.astype(jnp.float32)
    outs, Cs, ns, ms = [], [], [], []
    for d in range(2):
        li = jnp.swapaxes(pre[:, :, d, 0], 1, 2)
        lf = jnp.swapaxes(jax.nn.log_sigmoid(pre[:, :, d, 1]), 1, 2)
        if d == 0:
            h, C, n, m = mlstm_chunkwise(qh, kh, vh, li, lf, C0[:, d], n0[:, d], m0[:, d])
        else:
            h, C, n, m = mlstm_chunkwise(jnp.flip(qh, 2), jnp.flip(kh, 2), jnp.flip(vh, 2),
                                         jnp.flip(li, -1), jnp.flip(lf, -1), C0[:, d], n0[:, d], m0[:, d])
            h = jnp.flip(h, 2)
        outs.append(h)
        Cs.append(C)
        ns.append(n)
        ms.append(m)
    h = jnp.swapaxes(outs[0] + outs[1], 1, 2)
    h = rmsnorm(h, g_out.reshape(H, D)) * jax.nn.sigmoid(o.astype(jnp.float32))
    states = (jnp.stack(Cs, axis=1), jnp.stack(ns, axis=1), jnp.stack(ms, axis=1))
    return h.reshape(B, N, H * D).astype(v.dtype), states


def project_mixer_inputs(h, w_in, g_q, g_k):
    B, N, _ = h.shape
    aq, ak, av, bq, bk, bv, bo, bg, cq, ck, cv = jnp.split(h @ w_in, _split_points(), axis=-1)
    aq = rmsnorm(aq.reshape(B, N, A_HEADS, HEAD_DIM), g_q)
    ak = rmsnorm(ak.reshape(B, N, A_KV_HEADS, HEAD_DIM), g_k)
    av = av.reshape(B, N, A_KV_HEADS, HEAD_DIM)
    hb = lambda t: t.reshape(B, N, B_HEADS, HEAD_DIM)
    hc = lambda t: t.reshape(B, N, C_HEADS, HEAD_DIM)
    return (aq, ak, av, hb(bq), hb(bk), hb(bv), hb(bo), bg.reshape(B, N, 2, 2, B_HEADS),
            hc(cq), hc(ck), hc(cv))


def mixer_context(h, w_in, w_out, g_q, g_k, gate_bias, g_mout):
    B, N, _ = h.shape
    aq, ak, av, bq, bk, bv, bo, bg, cq, ck, cv = project_mixer_inputs(h, w_in, g_q, g_k)
    a_out = block_attention((aq * ATTN_SCALE).reshape(B, N, A_KV_HEADS, A_GROUP, HEAD_DIM), ak, av)
    C0 = jnp.zeros((B, 2, B_HEADS, HEAD_DIM, HEAD_DIM), jnp.float32)
    n0 = jnp.zeros((B, 2, B_HEADS, HEAD_DIM), jnp.float32)
    m0 = jnp.zeros((B, 2, B_HEADS), jnp.float32)
    b_out, (C, n, m) = mlstm_mixer(bq, bk, bv, bo, bg, gate_bias, g_mout, C0, n0, m0)
    c_out = block_attention((cq * ATTN_SCALE)[:, :, :, None, :], ck, cv)
    out = jnp.concatenate([a_out, b_out, c_out], axis=-1) @ w_out
    return out, (ak, av, ck, cv, C, n, m)


def mixer_latent(h, k_a_ctx, v_a_ctx, k_c_ctx, v_c_ctx, C0, n0, m0, w_in, w_out, g_q, g_k, gate_bias, g_mout, rpb):
    B, N, _ = h.shape
    aq, ak, av, bq, bk, bv, bo, bg, cq, ck, cv = project_mixer_inputs(h, w_in, g_q, g_k)
    cos, sin = axial_rope_tables(N)
    q_a = (apply_axial_rope(aq, cos, sin) * ATTN_SCALE).reshape(B, N, A_KV_HEADS, A_GROUP, HEAD_DIM)
    k_a = jnp.concatenate([apply_axial_rope(ak, cos, sin), k_a_ctx.astype(ak.dtype)], axis=1)
    v_a = jnp.concatenate([av, v_a_ctx.astype(av.dtype)], axis=1)
    a_out = block_attention(q_a, k_a, v_a)
    b_out, _ = mlstm_mixer(bq, bk, bv, bo, bg, gate_bias, g_mout, C0, n0, m0)
    c_out = neighbourhood_attention(cq * ATTN_SCALE, ck, cv, k_c_ctx, v_c_ctx, rpb)
    return jnp.concatenate([a_out, b_out, c_out], axis=-1) @ w_out


def swiglu(h, wg, wu, wd):
    return (jax.nn.silu(h @ wg) * (h @ wu)) @ wd


def moe_swiglu(h, w_router, wg, wu, wd):
    logits = jnp.einsum('bnd,de->bne', h, w_router, preferred_element_type=jnp.float32)
    top_v, top_i = lax.top_k(logits, TOP_K)
    top_w = jax.nn.softmax(top_v, axis=-1)
    gates = jnp.sum(jax.nn.one_hot(top_i, N_EXPERTS, dtype=jnp.float32) * top_w[..., None], axis=-2)
    y = gates[..., 0:1] * swiglu(h, wg[0], wu[0], wd[0]).astype(jnp.float32)
    for e in range(1, N_EXPERTS):
        y = y + gates[..., e:e + 1] * swiglu(h, wg[e], wu[e], wd[e]).astype(jnp.float32)
    return y.astype(h.dtype)


def channel_mixer(h, layer, w_ffn_gate, w_ffn_up, w_ffn_down, w_router, w_exp_gate, w_exp_up, w_exp_down):
    i = layer // 2
    if layer % 2 == 0:
        return swiglu(h, w_ffn_gate[i], w_ffn_up[i], w_ffn_down[i])
    return moe_swiglu(h, w_router[i], w_exp_gate[i], w_exp_up[i], w_exp_down[i])


def adaln_modulation(cvec, w, b):
    m = jax.nn.silu(cvec) @ w + b
    return m.reshape(cvec.shape[0], 1, 6, D_MODEL)


def modulate(hn, mod, i):
    return hn * (1 + mod[:, :, i + 1]) + mod[:, :, i]


def setup_inputs(seed: int = 0) -> dict:
    key = jax.random.key(seed)
    ks = iter(jax.random.split(key, 40))
    f32 = jnp.float32
    nrm = lambda shape, scale=1.0: scale * jax.random.normal(next(ks), shape, f32)
    gain = lambda shape: 1.0 + 0.05 * jax.random.normal(next(ks), shape, f32)
    gate_offset = jnp.array([0.0, MLSTM_F_BIAS], f32).reshape(1, 1, 2, 1)
    return {
        "x_prompt": nrm((BATCH, SEQ, D_MODEL)),
        "x_sample": nrm((DEC_BATCH, DEC_SEQ, D_MODEL)),
        "cache_gqa_k": nrm((DEC_BATCH, DEPTH, PAST_LEN, A_KV_HEADS, HEAD_DIM)),
        "cache_gqa_v": nrm((DEC_BATCH, DEPTH, PAST_LEN, A_KV_HEADS, HEAD_DIM)),
        "cache_na_k": nrm((DEC_BATCH, DEPTH, PAST_LEN, C_HEADS, HEAD_DIM)),
        "cache_na_v": nrm((DEC_BATCH, DEPTH, PAST_LEN, C_HEADS, HEAD_DIM)),
        "state_mlstm_C": nrm((DEC_BATCH, DEPTH, 2, B_HEADS, HEAD_DIM, HEAD_DIM), 0.3),
        "state_mlstm_n": nrm((DEC_BATCH, DEPTH, 2, B_HEADS, HEAD_DIM), 0.3),
        "state_mlstm_m": nrm((DEC_BATCH, DEPTH, 2, B_HEADS)),
        "c": nrm((DEC_BATCH, D_MODEL)),
        "c_ctx": nrm((D_MODEL,)),
        "w_mod": nrm((DEPTH, D_MODEL, 6 * D_MODEL), D_MODEL ** -0.5),
        "b_mod": nrm((DEPTH, 6 * D_MODEL), 0.02),
        "g_pre_mix": gain((DEPTH, D_MODEL)),
        "g_post_mix": gain((DEPTH, D_MODEL)),
        "g_pre_ffn": gain((DEPTH, D_MODEL)),
        "g_post_ffn": gain((DEPTH, D_MODEL)),
        "w_in": nrm((DEPTH, D_MODEL, P_IN), D_MODEL ** -0.5),
        "w_out": nrm((DEPTH, MIX_WIDTH, D_MODEL), MIX_WIDTH ** -0.5),
        "g_q": gain((DEPTH, HEAD_DIM)),
        "g_k": gain((DEPTH, HEAD_DIM)),
        "mlstm_gate_bias": gate_offset + nrm((DEPTH, 2, 2, B_HEADS), 0.3),
        "g_mlstm_out": gain((DEPTH, B_W)),
        "na_rpb": nrm((DEPTH, C_HEADS, 2 * NA_WIN_R - 1, 2 * NA_WIN_C - 1), 0.1),
        "w_ffn_gate": nrm((N_DENSE, D_MODEL, D_FF), D_MODEL ** -0.5),
        "w_ffn_up": nrm((N_DENSE, D_MODEL, D_FF), D_MODEL ** -0.5),
        "w_ffn_down": nrm((N_DENSE, D_FF, D_MODEL), D_FF ** -0.5),
        "w_router": nrm((N_MOE, D_MODEL, N_EXPERTS), D_MODEL ** -0.5),
        "w_exp_gate": nrm((N_MOE, N_EXPERTS, D_MODEL, D_FF_EXPERT), D_MODEL ** -0.5),
        "w_exp_up": nrm((N_MOE, N_EXPERTS, D_MODEL, D_FF_EXPERT), D_MODEL ** -0.5),
        "w_exp_down": nrm((N_MOE, N_EXPERTS, D_FF_EXPERT, D_MODEL), D_FF_EXPERT ** -0.5),
    }


def reference(x_prompt, x_sample, cache_gqa_k, cache_gqa_v, cache_na_k, cache_na_v,
              state_mlstm_C, state_mlstm_n, state_mlstm_m, c, c_ctx,
              w_mod, b_mod, g_pre_mix, g_post_mix, g_pre_ffn, g_post_ffn, w_in, w_out, g_q, g_k,
              mlstm_gate_bias, g_mlstm_out, na_rpb, w_ffn_gate, w_ffn_up, w_ffn_down,
              w_router, w_exp_gate, w_exp_up, w_exp_down):
    xp, xs = x_prompt, x_sample
    ka_l, va_l, kc_l, vc_l, C_l, n_l, m_l = [], [], [], [], [], [], []
    for l in range(DEPTH):
        mod_p = adaln_modulation(c_ctx[None, :], w_mod[l], b_mod[l])
        mod_s = adaln_modulation(c, w_mod[l], b_mod[l])
        h = modulate(rmsnorm(xp, g_pre_mix[l]), mod_p, 0)
        out, (ka, va, kc, vc, Cst, nst, mst) = mixer_context(h, w_in[l], w_out[l], g_q[l], g_k[l],
                                                             mlstm_gate_bias[l], g_mlstm_out[l])
        xp = xp + mod_p[:, :, 2] * rmsnorm(out, g_post_mix[l])
        h = modulate(rmsnorm(xp, g_pre_ffn[l]), mod_p, 3)
        f = channel_mixer(h, l, w_ffn_gate, w_ffn_up, w_ffn_down, w_router, w_exp_gate, w_exp_up, w_exp_down)
        xp = xp + mod_p[:, :, 5] * rmsnorm(f, g_post_ffn[l])
        ka_l.append(ka)
        va_l.append(va)
        kc_l.append(kc)
        vc_l.append(vc)
        C_l.append(Cst)
        n_l.append(nst)
        m_l.append(mst)
        h = modulate(rmsnorm(xs, g_pre_mix[l]), mod_s, 0)
        out = mixer_latent(h, cache_gqa_k[:, l], cache_gqa_v[:, l], cache_na_k[:, l], cache_na_v[:, l],
                           state_mlstm_C[:, l], state_mlstm_n[:, l], state_mlstm_m[:, l],
                           w_in[l], w_out[l], g_q[l], g_k[l], mlstm_gate_bias[l], g_mlstm_out[l], na_rpb[l])
        xs = xs + mod_s[:, :, 2] * rmsnorm(out, g_post_mix[l])
        h = modulate(rmsnorm(xs, g_pre_ffn[l]), mod_s, 3)
        f = channel_mixer(h, l, w_ffn_gate, w_ffn_up, w_ffn_down, w_router, w_exp_gate, w_exp_up, w_exp_down)
        xs = xs + mod_s[:, :, 5] * rmsnorm(f, g_post_ffn[l])
    new_gqa_k = jnp.stack(ka_l, axis=1)
    new_gqa_v = jnp.stack(va_l, axis=1)
    new_na_k = jnp.stack(kc_l, axis=1)
    new_na_v = jnp.stack(vc_l, axis=1)
    new_mlstm_C = jnp.stack(C_l, axis=1)
    new_mlstm_n = jnp.stack(n_l, axis=1)
    new_mlstm_m = jnp.stack(m_l, axis=1)
    return (xp, xs, new_gqa_k, new_gqa_v, new_na_k, new_na_v, new_mlstm_C, new_mlstm_n, new_mlstm_m)
```

```python
import functools

import jax
import jax.numpy as jnp
from jax import lax
from jax.experimental import pallas as pl
from jax.experimental.pallas import tpu as pltpu

D_MODEL = 1024
DEPTH = 2
GRID_W = 64
HEAD_DIM = 64
A_HEADS = 6
A_KV_HEADS = 2
A_GROUP = A_HEADS // A_KV_HEADS
B_HEADS = 4
C_HEADS = 6
A_Q_W = A_HEADS * HEAD_DIM
A_KV_W = A_KV_HEADS * HEAD_DIM
B_W = B_HEADS * HEAD_DIM
C_W = C_HEADS * HEAD_DIM
MLSTM_CHUNK = 64
NA_WIN_R = 8
NA_WIN_C = 16
ROPE_THETA = 10000.0
ROPE_QUARTER = HEAD_DIM // 4
ATTN_SCALE = HEAD_DIM ** -0.5
N_EXPERTS = 8
EPS = 1e-6

LANES = 128
N_GATES = 4 * B_HEADS
MASKED = -1e30

_OFF_AQ = 0
_OFF_AK = _OFF_AQ + A_Q_W
_OFF_AV = _OFF_AK + A_KV_W
_OFF_BQ = _OFF_AV + A_KV_W
_OFF_BK = _OFF_BQ + B_W
_OFF_BV = _OFF_BK + B_W
_OFF_BO = _OFF_BV + B_W
_OFF_CQ = _OFF_BO + B_W
_OFF_CK = _OFF_CQ + C_W
_OFF_CV = _OFF_CK + C_W
_OFF_BG = _OFF_CV + C_W
P_IN_PAD = _OFF_BG + LANES

TOKEN_TILE = 512
NA_Q_ROWS = 8
NA_K_ROWS = 16
VMEM_LIMIT = 56 * 1024 * 1024

F32 = jnp.float32
BF16 = jnp.bfloat16
HIGHEST = lax.Precision.HIGHEST
_NT = (((1,), (1,)), ((), ()))


def _params(*sem):
    return pltpu.CompilerParams(dimension_semantics=sem, vmem_limit_bytes=VMEM_LIMIT)


def _rms(x, g):
    return x * lax.rsqrt(jnp.mean(x * x, axis=-1, keepdims=True) + EPS) * g


def _silu(x):
    return x * jax.nn.sigmoid(x)


def _mod_kernel(c_ref, w_ref, b_ref, o_ref):
    s = _silu(c_ref[...])
    o_ref[0] = jnp.dot(s, w_ref[0], precision=HIGHEST, preferred_element_type=F32) + b_ref[0]


def _modulation(cvecs, w_mod, b_mod):
    nb = cvecs.shape[0]
    tn = 1536
    out = pl.pallas_call(
        _mod_kernel,
        grid=(DEPTH, 6 * D_MODEL // tn),
        in_specs=[pl.BlockSpec((nb, D_MODEL), lambda l, j: (0, 0)),
                  pl.BlockSpec((1, D_MODEL, tn), lambda l, j: (l, 0, j)),
                  pl.BlockSpec((1, 1, tn), lambda l, j: (l, 0, j))],
        out_specs=pl.BlockSpec((1, nb, tn), lambda l, j: (l, 0, j)),
        out_shape=jax.ShapeDtypeStruct((DEPTH, nb, 6 * D_MODEL), F32),
        compiler_params=_params("parallel", "parallel"),
        name="adaln_mod",
    )(cvecs, w_mod, b_mod.reshape(DEPTH, 1, 6 * D_MODEL))
    return out.reshape(DEPTH, nb, 6, D_MODEL)


def _inproj_kernel(*refs, rope, caches):
    x_ref, mod_ref, g_ref, w_ref, gq_ref, gk_ref = refs[:6]
    pos = 6
    if rope:
        cos_ref, sin_ref = refs[pos:pos + 2]
        pos += 2
    (qa_ref, ka_ref, va_ref, qb_ref, kb_ref, vb_ref, ob_ref, gb_ref,
     qc_ref, kc_ref, vc_ref) = refs[pos:pos + 11]
    pos += 11
    if caches:
        cka_ref, cva_ref, ckc_ref, cvc_ref = refs[pos:pos + 4]

    m = mod_ref[0]
    h = _rms(x_ref[...], g_ref[...]) * (1.0 + m[1:2]) + m[0:1]
    y = jnp.dot(h.astype(BF16), w_ref[...], preferred_element_type=F32)

    def head(off, j):
        return y[:, off + j * HEAD_DIM: off + (j + 1) * HEAD_DIM]

    def rotary(p):
        if not rope:
            return p
        swapped = jnp.concatenate([p[:, 16:32], p[:, 0:16], p[:, 48:64], p[:, 32:48]], axis=1)
        return p * cos_ref[...] + swapped * sin_ref[...]

    for j in range(A_HEADS):
        qa_ref[j] = (rotary(_rms(head(_OFF_AQ, j), gq_ref[...])) * ATTN_SCALE).astype(BF16)
    for j in range(A_KV_HEADS):
        kn = _rms(head(_OFF_AK, j), gk_ref[...])
        ka_ref[j] = rotary(kn).astype(BF16)
        va_ref[j] = head(_OFF_AV, j).astype(BF16)
        if caches:
            cka_ref[:, j * HEAD_DIM:(j + 1) * HEAD_DIM] = kn
    lane = lax.broadcasted_iota(jnp.int32, (y.shape[0], LANES), 1)
    ones_col = jnp.where(lane == HEAD_DIM, 1.0, 0.0)
    for j in range(B_HEADS):
        qb_ref[j] = (head(_OFF_BQ, j) * ATTN_SCALE).astype(BF16)
        kb_ref[j] = head(_OFF_BK, j).astype(BF16)
        wide = y[:, _OFF_BV + j * HEAD_DIM: _OFF_BV + j * HEAD_DIM + LANES]
        vb_ref[j] = jnp.where(lane < HEAD_DIM, wide, ones_col).astype(BF16)
    ob_ref[...] = y[:, _OFF_BO:_OFF_BO + B_W]
    gb_ref[...] = y[:, _OFF_BG:_OFF_BG + LANES]
    for j in range(C_HEADS):
        qc_ref[j] = (head(_OFF_CQ, j) * ATTN_SCALE).astype(BF16)
        kc_ref[j] = head(_OFF_CK, j).astype(BF16)
        vc_ref[j] = head(_OFF_CV, j).astype(BF16)
    if caches:
        cva_ref[...] = y[:, _OFF_AV:_OFF_AV + A_KV_W]
        ckc_ref[...] = y[:, _OFF_CK:_OFF_CK + C_W]
        cvc_ref[...] = y[:, _OFF_CV:_OFF_CV + C_W]


def _in_projection(x, mod, mod_row, g_pre, w_in, g_q, g_k, rope_tabs, caches):
    t = x.shape[0]
    tm = TOKEN_TILE
    rope = rope_tabs is not None
    row = lambda i: (i, 0)
    heads = lambda i: (0, i, 0)
    fixed2 = lambda i: (0, 0)
    in_specs = [pl.BlockSpec((tm, D_MODEL), row),
                pl.BlockSpec((1, 6, D_MODEL), lambda i: (mod_row(i), 0, 0)),
                pl.BlockSpec((1, D_MODEL), fixed2),
                pl.BlockSpec((D_MODEL, P_IN_PAD), fixed2),
                pl.BlockSpec((1, HEAD_DIM), fixed2),
                pl.BlockSpec((1, HEAD_DIM), fixed2)]
    args = [x, mod, g_pre, w_in, g_q, g_k]
    if rope:
        n_pos = rope_tabs[0].shape[0] // tm
        in_specs += [pl.BlockSpec((tm, HEAD_DIM), lambda i: (i % n_pos, 0))] * 2
        args += list(rope_tabs)

    def hm(nh, width=HEAD_DIM):
        return (jax.ShapeDtypeStruct((nh, t, width), BF16), pl.BlockSpec((nh, tm, width), heads))

    def tokmajor(width):
        return (jax.ShapeDtypeStruct((t, width), F32), pl.BlockSpec((tm, width), row))

    outs = [hm(A_HEADS), hm(A_KV_HEADS), hm(A_KV_HEADS),
            hm(B_HEADS), hm(B_HEADS), hm(B_HEADS, LANES), tokmajor(B_W), tokmajor(LANES),
            hm(C_HEADS), hm(C_HEADS), hm(C_HEADS)]
    if caches:
        outs += [tokmajor(A_KV_W), tokmajor(A_KV_W), tokmajor(C_W), tokmajor(C_W)]
    return pl.pallas_call(
        functools.partial(_inproj_kernel, rope=rope, caches=caches),
        grid=(t // tm,),
        in_specs=in_specs,
        out_specs=[o[1] for o in outs],
        out_shape=[o[0] for o in outs],
        compiler_params=_params("parallel"),
        name="in_projection",
    )(*args)


def _attn_kernel(*refs, ctx):
    if ctx:
        q_ref, k_ref, v_ref, kx_ref, vx_ref, o_ref = refs
    else:
        q_ref, k_ref, v_ref, o_ref = refs
    g, tq, d = q_ref.shape
    q = q_ref[...].reshape(g * tq, d)
    s = lax.dot_general(q, k_ref[0], _NT, preferred_element_type=F32)
    mx = jnp.max(s, axis=-1, keepdims=True)
    if ctx:
        sx = lax.dot_general(q, kx_ref[0], _NT, preferred_element_type=F32)
        mx = jnp.maximum(mx, jnp.max(sx, axis=-1, keepdims=True))
    p = jnp.exp(s - mx)
    den = jnp.sum(p, axis=-1, keepdims=True)
    o = jnp.dot(p.astype(BF16), v_ref[0], preferred_element_type=F32)
    if ctx:
        px = jnp.exp(sx - mx)
        den = den + jnp.sum(px, axis=-1, keepdims=True)
        o = o + jnp.dot(px.astype(BF16), vx_ref[0], preferred_element_type=F32)
    o_ref[...] = (o / den).reshape(g, tq, d)


def _attention(q, k, v, n_seq, tq, ctx=None):
    hq, t, d = q.shape
    hkv = k.shape[0]
    g = hq // hkv
    nb = t // n_seq
    nq = n_seq // tq
    in_specs = [pl.BlockSpec((g, tq, d), lambda b, kv, i: (kv, b * nq + i, 0)),
                pl.BlockSpec((1, n_seq, d), lambda b, kv, i: (kv, b, 0)),
                pl.BlockSpec((1, n_seq, d), lambda b, kv, i: (kv, b, 0))]
    args = [q, k, v]
    if ctx is not None:
        lc = ctx[0].shape[1] // nb
        in_specs += [pl.BlockSpec((1, lc, d), lambda b, kv, i: (kv, b, 0))] * 2
        args += list(ctx)
    return pl.pallas_call(
        functools.partial(_attn_kernel, ctx=ctx is not None),
        grid=(nb, hkv, nq),
        in_specs=in_specs,
        out_specs=pl.BlockSpec((g, tq, d), lambda b, kv, i: (kv, b * nq + i, 0)),
        out_shape=jax.ShapeDtypeStruct((hq, t, d), F32),
        compiler_params=_params("parallel", "parallel", "parallel"),
        name="dense_attention",
    )(*args)


def _na_kernel(q_ref, k_ref, v_ref, kx_ref, vx_ref, bias_ref, o_ref):
    i = pl.program_id(1)
    tq = q_ref.shape[1]
    nk = bias_ref.shape[3]
    first_row = jnp.clip(NA_Q_ROWS * i - NA_WIN_R // 2, 0, k_ref.shape[1] // GRID_W - NA_K_ROWS)
    start = pl.multiple_of(first_row * GRID_W, GRID_W)
    q = q_ref[0]
    k = k_ref[0, pl.ds(start, nk), :]
    v = v_ref[0, pl.ds(start, nk), :]
    s = lax.dot_general(q, k, _NT, preferred_element_type=F32) + bias_ref[0, 0]
    sx = lax.dot_general(q, kx_ref[0], _NT, preferred_element_type=F32)
    mx = jnp.maximum(jnp.max(s, axis=-1, keepdims=True), jnp.max(sx, axis=-1, keepdims=True))
    p = jnp.exp(s - mx)
    px = jnp.exp(sx - mx)
    den = jnp.sum(p, axis=-1, keepdims=True) + jnp.sum(px, axis=-1, keepdims=True)
    o = (jnp.dot(p.astype(BF16), v, preferred_element_type=F32)
         + jnp.dot(px.astype(BF16), vx_ref[0], preferred_element_type=F32))
    o_ref[0] = o / den
    del tq


def _na_bias(rpb, rows):
    nblk = rows // NA_Q_ROWS
    blk = jnp.arange(nblk, dtype=jnp.int32)[:, None, None, None, None]
    qa = jnp.arange(NA_Q_ROWS, dtype=jnp.int32)[None, :, None, None, None]
    qc = jnp.arange(GRID_W, dtype=jnp.int32)[None, None, :, None, None]
    kw = jnp.arange(NA_K_ROWS, dtype=jnp.int32)[None, None, None, :, None]
    kc = jnp.arange(GRID_W, dtype=jnp.int32)[None, None, None, None, :]
    qr = blk * NA_Q_ROWS + qa
    kr = jnp.clip(blk * NA_Q_ROWS - NA_WIN_R // 2, 0, rows - NA_K_ROWS) + kw
    r_start = jnp.clip(qr - NA_WIN_R // 2, 0, rows - NA_WIN_R)
    c_start = jnp.clip(qc - NA_WIN_C // 2, 0, GRID_W - NA_WIN_C)
    valid = (kr >= r_start) & (kr < r_start + NA_WIN_R) & (kc >= c_start) & (kc < c_start + NA_WIN_C)
    idx = jnp.where(valid, (kr - qr + NA_WIN_R - 1) * (2 * NA_WIN_C - 1) + (kc - qc + NA_WIN_C - 1), 0)
    idx = jnp.broadcast_to(idx, (nblk, NA_Q_ROWS, GRID_W, NA_K_ROWS, GRID_W))
    valid = jnp.broadcast_to(valid, idx.shape)
    flat = rpb.astype(F32).reshape(rpb.shape[0], -1)
    bias = jnp.where(valid[None], jnp.take(flat, idx, axis=1), MASKED)
    return bias.reshape(rpb.shape[0], nblk, NA_Q_ROWS * GRID_W, NA_K_ROWS * GRID_W)


def _neighbourhood_attention(q, k, v, kx, vx, bias, n_seq):
    h, t, d = q.shape
    nb = t // n_seq
    tq = NA_Q_ROWS * GRID_W
    nblk = n_seq // tq
    lc = kx.shape[1] // nb
    return pl.pallas_call(
        _na_kernel,
        grid=(h, nblk, nb),
        in_specs=[pl.BlockSpec((1, tq, d), lambda hh, i, b: (hh, b * nblk + i, 0)),
                  pl.BlockSpec((1, n_seq, d), lambda hh, i, b: (hh, b, 0)),
                  pl.BlockSpec((1, n_seq, d), lambda hh, i, b: (hh, b, 0)),
                  pl.BlockSpec((1, lc, d), lambda hh, i, b: (hh, b, 0)),
                  pl.BlockSpec((1, lc, d), lambda hh, i, b: (hh, b, 0)),
                  pl.BlockSpec((1, 1, tq, NA_K_ROWS * GRID_W), lambda hh, i, b: (hh, i, 0, 0))],
        out_specs=pl.BlockSpec((1, tq, d), lambda hh, i, b: (hh, b * nblk + i, 0)),
        out_shape=jax.ShapeDtypeStruct((h, t, d), F32),
        compiler_params=_params("parallel", "parallel", "parallel"),
        name="neighbourhood_attention",
    )(q, k, v, kx, vx, bias)


def _mlstm_kernel(q_ref, k_ref, v_ref, o_ref, g_ref, gbias_ref, gout_ref, c0_ref, m0_ref,
                  h_ref, cn_ref, mn_ref, hs_ref, c_scr, m_scr):
    L = MLSTM_CHUNK
    n = q_ref.shape[1]
    nc = n // L
    hs_ref[...] = jnp.zeros(hs_ref.shape, F32)
    c_scr[...] = c0_ref[0]
    m_scr[...] = m0_ref[0]

    row = lax.broadcasted_iota(jnp.int32, (L, L), 0)
    col = lax.broadcasted_iota(jnp.int32, (L, L), 1)
    masks = (col <= row, col >= row)
    cum_mats = tuple(mk.astype(F32) for mk in masks)
    lane = lax.broadcasted_iota(jnp.int32, (L, LANES), 1)
    eye_l = (lax.broadcasted_iota(jnp.int32, (LANES, LANES), 0)
             == lax.broadcasted_iota(jnp.int32, (LANES, LANES), 1)).astype(F32)
    eye_k = (row == col).astype(BF16)

    def column(xmat, j):
        return jnp.sum(jnp.where(lane == j, xmat, 0.0), axis=1, keepdims=True)

    def body(i, carry):
        for d in range(2):
            c = i if d == 0 else nc - 1 - i
            rows = pl.ds(pl.multiple_of(c * L, L), L)
            pre = g_ref[rows, :] + gbias_ref[...]
            logf = jnp.minimum(pre, 0.0) - jnp.log1p(jnp.exp(-jnp.abs(pre)))
            cum = jnp.dot(cum_mats[d], logf, precision=HIGHEST, preferred_element_type=F32)
            pre_t = lax.dot_general(eye_l, pre, _NT, precision=HIGHEST, preferred_element_type=F32)
            cum_t = lax.dot_general(eye_l, cum, _NT, precision=HIGHEST, preferred_element_type=F32)
            last = L - 1 if d == 0 else 0
            for hh in range(B_HEADS):
                s_idx = d * B_HEADS + hh
                ji = d * 2 * B_HEADS + hh
                jf = ji + B_HEADS
                b_col = column(cum, jf)
                li_col = column(pre, ji)
                b_row = cum_t[jf:jf + 1, :]
                li_row = pre_t[ji:ji + 1, :]
                m_prev = m_scr[s_idx:s_idx + 1, 0:1]
                c_aug = c_scr[s_idx]
                log_d = jnp.where(masks[d], b_col - b_row + li_row, -jnp.inf)
                m_inter = b_col + m_prev
                m_t = jnp.maximum(m_inter, jnp.max(log_d, axis=-1, keepdims=True))
                w_inter = jnp.exp(m_inter - m_t)
                qc = q_ref[hh, rows, :]
                kc = k_ref[hh, rows, :]
                vc = v_ref[hh, rows, :]
                s = lax.dot_general(qc, kc, _NT, preferred_element_type=F32) * jnp.exp(log_d - m_t)
                num = (jnp.dot(s.astype(BF16), vc, preferred_element_type=F32)
                       + w_inter * jnp.dot(qc, c_aug.astype(BF16), preferred_element_type=F32))
                den = column(num, HEAD_DIM)
                hs_ref[hh, rows, :] += num[:, :HEAD_DIM] / jnp.maximum(jnp.abs(den), jnp.exp(-m_t))
                m_new = m_t[last:last + 1, :]
                b_last = b_col[last:last + 1, :]
                w = jnp.exp(b_last - b_col + li_col - m_new)
                decay = jnp.exp(b_last + m_prev - m_new)
                k_t = lax.dot_general(eye_k, kc, _NT, preferred_element_type=F32).astype(BF16)
                wv = (w * vc.astype(F32)).astype(BF16)
                c_scr[s_idx] = decay * c_aug + jnp.dot(k_t, wv, preferred_element_type=F32)
                m_scr[s_idx:s_idx + 1, :] = jnp.broadcast_to(m_new, (1, LANES))
        return carry

    lax.fori_loop(0, nc, body, 0)

    for hh in range(B_HEADS):
        gate = jax.nn.sigmoid(o_ref[:, hh * HEAD_DIM:(hh + 1) * HEAD_DIM])
        h_ref[hh] = _rms(hs_ref[hh], gout_ref[hh]) * gate
    cn_ref[0] = c_scr[...]
    mn_ref[0] = m_scr[...]


def _mlstm(q, k, v, o, g, gate_bias, g_out, c0, m0, n_seq):
    h, t, d = q.shape
    nb = t // n_seq
    ns = 2 * h
    heads = lambda b: (0, b, 0)
    return pl.pallas_call(
        _mlstm_kernel,
        grid=(nb,),
        in_specs=[pl.BlockSpec((h, n_seq, d), heads),
                  pl.BlockSpec((h, n_seq, d), heads),
                  pl.BlockSpec((h, n_seq, LANES), heads),
                  pl.BlockSpec((n_seq, h * d), lambda b: (b, 0)),
                  pl.BlockSpec((n_seq, LANES), lambda b: (b, 0)),
                  pl.BlockSpec((1, LANES), lambda b: (0, 0)),
                  pl.BlockSpec((h, 1, d), lambda b: (0, 0, 0)),
                  pl.BlockSpec((1, ns, d, LANES), lambda b: (b, 0, 0, 0)),
                  pl.BlockSpec((1, ns, LANES), lambda b: (b, 0, 0))],
        out_specs=[pl.BlockSpec((h, n_seq, d), heads),
                   pl.BlockSpec((1, ns, d, LANES), lambda b: (b, 0, 0, 0)),
                   pl.BlockSpec((1, ns, LANES), lambda b: (b, 0, 0))],
        out_shape=[jax.ShapeDtypeStruct((h, t, d), F32),
                   jax.ShapeDtypeStruct((nb, ns, d, LANES), F32),
                   jax.ShapeDtypeStruct((nb, ns, LANES), F32)],
        scratch_shapes=[pltpu.VMEM((h, n_seq, d), F32),
                        pltpu.VMEM((ns, d, LANES), F32),
                        pltpu.VMEM((ns, LANES), F32)],
        compiler_params=_params("parallel"),
        name="mlstm",
    )(q, k, v, o, g, gate_bias, g_out, c0, m0)


def _outproj_kernel(*refs, moe):
    a_ref, b_ref, c_ref, w_ref, x_ref, mod_ref, gpost_ref, gpre_ref = refs[:8]
    pos = 8
    if moe:
        wr_ref = refs[pos]
        pos += 1
    xo_ref, h2_ref = refs[pos:pos + 2]
    pos += 2
    if moe:
        gates_ref = refs[pos]
        pos += 1
    cat_ref = refs[pos]

    off = 0
    for src in (a_ref, b_ref, c_ref):
        for j in range(src.shape[0]):
            cat_ref[:, off:off + HEAD_DIM] = src[j]
            off += HEAD_DIM
    out = jnp.dot(cat_ref[...].astype(BF16), w_ref[...], preferred_element_type=F32)
    m = mod_ref[0]
    xn = x_ref[...] + m[2:3] * _rms(out, gpost_ref[...])
    xo_ref[...] = xn
    h2 = _rms(xn, gpre_ref[...]) * (1.0 + m[4:5]) + m[3:4]
    h2_ref[...] = h2.astype(BF16)
    if moe:
        logits = jnp.dot(h2, wr_ref[...], precision=HIGHEST, preferred_element_type=F32)
        lane = lax.broadcasted_iota(jnp.int32, logits.shape, 1)
        logits = jnp.where(lane < N_EXPERTS, logits, -jnp.inf)
        v1 = jnp.max(logits, axis=-1, keepdims=True)
        i1 = jnp.min(jnp.where(logits == v1, lane, LANES), axis=-1, keepdims=True)
        rest = jnp.where(lane == i1, -jnp.inf, logits)
        v2 = jnp.max(rest, axis=-1, keepdims=True)
        i2 = jnp.min(jnp.where(rest == v2, lane, LANES), axis=-1, keepdims=True)
        e2 = jnp.exp(v2 - v1)
        w1 = 1.0 / (1.0 + e2)
        w2 = e2 / (1.0 + e2)
        gates_ref[...] = jnp.where(lane == i1, w1, 0.0) + jnp.where(lane == i2, w2, 0.0)


def _out_projection(a, b, c, w_out, x, mod, mod_row, g_post, g_pre, w_router):
    t = x.shape[0]
    tm = TOKEN_TILE
    moe = w_router is not None
    row = lambda i: (i, 0)
    fixed2 = lambda i: (0, 0)
    heads = lambda i: (0, i, 0)
    in_specs = [pl.BlockSpec((a.shape[0], tm, HEAD_DIM), heads),
                pl.BlockSpec((b.shape[0], tm, HEAD_DIM), heads),
                pl.BlockSpec((c.shape[0], tm, HEAD_DIM), heads),
                pl.BlockSpec((D_MODEL, D_MODEL), fixed2),
                pl.BlockSpec((tm, D_MODEL), row),
                pl.BlockSpec((1, 6, D_MODEL), lambda i: (mod_row(i), 0, 0)),
                pl.BlockSpec((1, D_MODEL), fixed2),
                pl.BlockSpec((1, D_MODEL), fixed2)]
    args = [a, b, c, w_out, x, mod, g_post, g_pre]
    out_specs = [pl.BlockSpec((tm, D_MODEL), row), pl.BlockSpec((tm, D_MODEL), row)]
    out_shape = [jax.ShapeDtypeStruct((t, D_MODEL), F32), jax.ShapeDtypeStruct((t, D_MODEL), BF16)]
    if moe:
        in_specs.append(pl.BlockSpec((D_MODEL, LANES), fixed2))
        args.append(w_router)
        out_specs.append(pl.BlockSpec((tm, LANES), row))
        out_shape.append(jax.ShapeDtypeStruct((t, LANES), F32))
    return pl.pallas_call(
        functools.partial(_outproj_kernel, moe=moe),
        grid=(t // tm,),
        in_specs=in_specs,
        out_specs=out_specs,
        out_shape=out_shape,
        scratch_shapes=[pltpu.VMEM((tm, D_MODEL), F32)],
        compiler_params=_params("parallel"),
        name="out_projection",
    )(*args)


def _swiglu_chunk(h, wg, wu, wd):
    a = jnp.dot(h, wg, preferred_element_type=F32)
    b = jnp.dot(h, wu, preferred_element_type=F32)
    return jnp.dot((_silu(a) * b).astype(BF16), wd, preferred_element_type=F32)


def _ffn_kernel(h_ref, wg_ref, wu_ref, wd_ref, x_ref, mod_ref, g_ref, o_ref, acc_ref):
    j = pl.program_id(1)

    @pl.when(j == 0)
    def _():
        acc_ref[...] = jnp.zeros(acc_ref.shape, F32)

    acc_ref[...] += _swiglu_chunk(h_ref[...], wg_ref[...], wu_ref[...], wd_ref[...])

    @pl.when(j == pl.num_programs(1) - 1)
    def _():
        o_ref[...] = x_ref[...] + mod_ref[0][5:6] * _rms(acc_ref[...], g_ref[...])


def _dense_ffn(h2, wg, wu, wd, x, mod, mod_row, g_post):
    t = x.shape[0]
    tm = TOKEN_TILE
    ff = wg.shape[1]
    tf = ff // 2
    return pl.pallas_call(
        _ffn_kernel,
        grid=(t // tm, ff // tf),
        in_specs=[pl.BlockSpec((tm, D_MODEL), lambda i, j: (i, 0)),
                  pl.BlockSpec((D_MODEL, tf), lambda i, j: (0, j)),
                  pl.BlockSpec((D_MODEL, tf), lambda i, j: (0, j)),
                  pl.BlockSpec((tf, D_MODEL), lambda i, j: (j, 0)),
                  pl.BlockSpec((tm, D_MODEL), lambda i, j: (i, 0)),
                  pl.BlockSpec((1, 6, D_MODEL), lambda i, j: (mod_row(i), 0, 0)),
                  pl.BlockSpec((1, D_MODEL), lambda i, j: (0, 0))],
        out_specs=pl.BlockSpec((tm, D_MODEL), lambda i, j: (i, 0)),
        out_shape=jax.ShapeDtypeStruct((t, D_MODEL), F32),
        scratch_shapes=[pltpu.VMEM((tm, D_MODEL), F32)],
        compiler_params=_params("parallel", "arbitrary"),
        name="dense_ffn",
    )(h2, wg, wu, wd, x, mod, g_post)


def _moe_kernel(h_ref, gates_ref, wg_ref, wu_ref, wd_ref, x_ref, mod_ref, g_ref, o_ref, acc_ref):
    e = pl.program_id(1)
    j = pl.program_id(2)

    @pl.when((e == 0) & (j == 0))
    def _():
        acc_ref[...] = jnp.zeros(acc_ref.shape, F32)

    gates = gates_ref[...]
    lane = lax.broadcasted_iota(jnp.int32, gates.shape, 1)
    gate = jnp.sum(jnp.where(lane == e, gates, 0.0), axis=-1, keepdims=True)
    acc_ref[...] += gate * _swiglu_chunk(h_ref[...], wg_ref[0], wu_ref[0], wd_ref[0])

    @pl.when((e == pl.num_programs(1) - 1) & (j == pl.num_programs(2) - 1))
    def _():
        o_ref[...] = x_ref[...] + mod_ref[0][5:6] * _rms(acc_ref[...], g_ref[...])


def _moe_ffn(h2, gates, wg, wu, wd, x, mod, mod_row, g_post):
    t = x.shape[0]
    tm = 2 * TOKEN_TILE
    ne, _, ff = wg.shape
    tf = 512
    mrow = lambda i: mod_row(i * (tm // TOKEN_TILE))
    return pl.pallas_call(
        _moe_kernel,
        grid=(t // tm, ne, ff // tf),
        in_specs=[pl.BlockSpec((tm, D_MODEL), lambda i, e, j: (i, 0)),
                  pl.BlockSpec((tm, LANES), lambda i, e, j: (i, 0)),
                  pl.BlockSpec((1, D_MODEL, tf), lambda i, e, j: (e, 0, j)),
                  pl.BlockSpec((1, D_MODEL, tf), lambda i, e, j: (e, 0, j)),
                  pl.BlockSpec((1, tf, D_MODEL), lambda i, e, j: (e, j, 0)),
                  pl.BlockSpec((tm, D_MODEL), lambda i, e, j: (i, 0)),
                  pl.BlockSpec((1, 6, D_MODEL), lambda i, e, j: (mrow(i), 0, 0)),
                  pl.BlockSpec((1, D_MODEL), lambda i, e, j: (0, 0))],
        out_specs=pl.BlockSpec((tm, D_MODEL), lambda i, e, j: (i, 0)),
        out_shape=jax.ShapeDtypeStruct((t, D_MODEL), F32),
        scratch_shapes=[pltpu.VMEM((tm, D_MODEL), F32)],
        compiler_params=_params("parallel", "arbitrary", "arbitrary"),
        name="moe_ffn",
    )(h2, gates, wg, wu, wd, x, mod, g_post)


def _reorder_w_in(w):
    gate0 = _OFF_BO + B_W
    body = jnp.concatenate([w[:, :gate0], w[:, gate0 + N_GATES:]], axis=1)
    gates = w[:, gate0:gate0 + N_GATES]
    pad = jnp.zeros((w.shape[0], LANES - N_GATES), w.dtype)
    return jnp.concatenate([body, gates, pad], axis=1).astype(BF16)


def _rope_tables(n):
    tok = jnp.arange(n, dtype=jnp.int32)
    pos = jnp.stack([tok // GRID_W, tok % GRID_W], axis=-1).astype(F32)
    inv = ROPE_THETA ** (-jnp.arange(ROPE_QUARTER, dtype=F32) / ROPE_QUARTER)
    ang = pos[:, :, None] * inv
    cos, sin = jnp.cos(ang), jnp.sin(ang)
    cos_t = jnp.concatenate([cos[:, 0], cos[:, 0], cos[:, 1], cos[:, 1]], axis=-1)
    sin_t = jnp.concatenate([-sin[:, 0], sin[:, 0], -sin[:, 1], sin[:, 1]], axis=-1)
    return cos_t, sin_t


def _head_major(cache):
    b, l, h, d = cache.shape
    return jnp.transpose(cache, (2, 0, 1, 3)).reshape(h, b * l, d).astype(BF16)


def kernel(x_prompt, x_sample, cache_gqa_k, cache_gqa_v, cache_na_k, cache_na_v, state_mlstm_C, state_mlstm_n, state_mlstm_m, c, c_ctx, w_mod, b_mod, g_pre_mix, g_post_mix, g_pre_ffn, g_post_ffn, w_in, w_out, g_q, g_k, mlstm_gate_bias, g_mlstm_out, na_rpb, w_ffn_gate, w_ffn_up, w_ffn_down, w_router, w_exp_gate, w_exp_up, w_exp_down):
    batch, seq, _ = x_prompt.shape
    dec_batch, dec_seq, _ = x_sample.shape
    assert seq % TOKEN_TILE == 0 or TOKEN_TILE % seq == 0
    assert dec_seq % (2 * TOKEN_TILE) == 0 and dec_seq % (NA_Q_ROWS * GRID_W) == 0
    tp, ts = batch * seq, dec_batch * dec_seq
    xp = x_prompt.reshape(tp, D_MODEL)
    xs = x_sample.reshape(ts, D_MODEL)

    n_mod = 8
    cvecs = jnp.concatenate([c_ctx[None, :], c, jnp.zeros((n_mod - 1 - dec_batch, D_MODEL), F32)], axis=0)
    mod_all = _modulation(cvecs, w_mod, b_mod)
    prompt_row = lambda i: 0
    tiles_per_sample = dec_seq // TOKEN_TILE
    sample_row = lambda i: 1 + i // tiles_per_sample

    rope_tabs = _rope_tables(dec_seq)
    row2 = lambda v: v.reshape(1, -1)
    ns = 2 * B_HEADS
    zeros_c = jnp.zeros((batch, ns, HEAD_DIM, LANES), F32)
    zeros_m = jnp.zeros((batch, ns, LANES), F32)

    ka_l, va_l, kc_l, vc_l, cst_l, m_l = [], [], [], [], [], []
    for l in range(DEPTH):
        mod = mod_all[l]
        w_in_l = _reorder_w_in(w_in[l])
        w_out_l = w_out[l].astype(BF16)
        gbias = jnp.concatenate([mlstm_gate_bias[l].reshape(1, N_GATES),
                                 jnp.zeros((1, LANES - N_GATES), F32)], axis=1)
        gout = g_mlstm_out[l].reshape(B_HEADS, 1, HEAD_DIM)
        moe = l % 2 == 1
        if moe:
            w_r = jnp.concatenate([w_router[l // 2], jnp.zeros((D_MODEL, LANES - N_EXPERTS), F32)], axis=1)
            ffn_w = tuple(w[l // 2].astype(BF16) for w in (w_exp_gate, w_exp_up, w_exp_down))
        else:
            w_r = None
            ffn_w = tuple(w[l // 2].astype(BF16) for w in (w_ffn_gate, w_ffn_up, w_ffn_down))

        def channel_mixer(res, x_mid, mod_row):
            if moe:
                return _moe_ffn(res[1], res[2], *ffn_w, x_mid, mod, mod_row, row2(g_post_ffn[l]))
            return _dense_ffn(res[1], *ffn_w, x_mid, mod, mod_row, row2(g_post_ffn[l]))

        (qa, ka, va, qb, kb, vb, ob, gb, qc, kc, vc, cka, cva, ckc, cvc) = _in_projection(
            xp, mod, prompt_row, row2(g_pre_mix[l]), w_in_l, row2(g_q[l]), row2(g_k[l]), None, True)
        a_out = _attention(qa, ka, va, seq, seq)
        b_out, cst, mst = _mlstm(qb, kb, vb, ob, gb, gbias, gout, zeros_c, zeros_m, seq)
        c_out = _attention(qc, kc, vc, seq, seq)
        res = _out_projection(a_out, b_out, c_out, w_out_l, xp, mod, prompt_row,
                              row2(g_post_mix[l]), row2(g_pre_ffn[l]), w_r)
        xp = channel_mixer(res, res[0], prompt_row)
        ka_l.append(cka.reshape(batch, seq, A_KV_HEADS, HEAD_DIM))
        va_l.append(cva.reshape(batch, seq, A_KV_HEADS, HEAD_DIM))
        kc_l.append(ckc.reshape(batch, seq, C_HEADS, HEAD_DIM))
        vc_l.append(cvc.reshape(batch, seq, C_HEADS, HEAD_DIM))
        cst_l.append(cst.reshape(batch, 2, B_HEADS, HEAD_DIM, LANES))
        m_l.append(mst.reshape(batch, 2, B_HEADS, LANES)[..., 0])

        (qa, ka, va, qb, kb, vb, ob, gb, qc, kc, vc) = _in_projection(
            xs, mod, sample_row, row2(g_pre_mix[l]), w_in_l, row2(g_q[l]), row2(g_k[l]), rope_tabs, False)
        a_out = _attention(qa, ka, va, dec_seq, 256,
                           ctx=(_head_major(cache_gqa_k[:, l]), _head_major(cache_gqa_v[:, l])))
        c0 = jnp.concatenate([state_mlstm_C[:, l], state_mlstm_n[:, l][..., None],
                              jnp.zeros((dec_batch, 2, B_HEADS, HEAD_DIM, LANES - HEAD_DIM - 1), F32)],
                             axis=-1).reshape(dec_batch, ns, HEAD_DIM, LANES)
        m0 = jnp.broadcast_to(state_mlstm_m[:, l].reshape(dec_batch, ns, 1), (dec_batch, ns, LANES))
        b_out, _, _ = _mlstm(qb, kb, vb, ob, gb, gbias, gout, c0, m0, dec_seq)
        c_out = _neighbourhood_attention(qc, kc, vc, _head_major(cache_na_k[:, l]),
                                         _head_major(cache_na_v[:, l]),
                                         _na_bias(na_rpb[l], dec_seq // GRID_W), dec_seq)
        res = _out_projection(a_out, b_out, c_out, w_out_l, xs, mod, sample_row,
                              row2(g_post_mix[l]), row2(g_pre_ffn[l]), w_r)
        xs = channel_mixer(res, res[0], sample_row)

    cst = jnp.stack(cst_l, axis=1)
    return (xp.reshape(batch, seq, D_MODEL), xs.reshape(dec_batch, dec_seq, D_MODEL),
            jnp.stack(ka_l, axis=1), jnp.stack(va_l, axis=1),
            jnp.stack(kc_l, axis=1), jnp.stack(vc_l, axis=1),
            cst[..., :HEAD_DIM], cst[..., HEAD_DIM], jnp.stack(m_l, axis=1))
```

```python
import functools

import jax
import jax.numpy as jnp
from jax import lax
from jax.experimental import pallas as pl
from jax.experimental.pallas import tpu as pltpu

D_MODEL = 1024
DEPTH = 2
GRID_W = 64
HEAD_DIM = 64
A_HEADS = 6
A_KV_HEADS = 2
A_GROUP = A_HEADS // A_KV_HEADS
B_HEADS = 4
C_HEADS = 6
A_Q_W = A_HEADS * HEAD_DIM
A_KV_W = A_KV_HEADS * HEAD_DIM
B_W = B_HEADS * HEAD_DIM
C_W = C_HEADS * HEAD_DIM
MLSTM_CHUNK = 64
NA_WIN_R = 8
NA_WIN_C = 16
ROPE_THETA = 10000.0
ROPE_QUARTER = HEAD_DIM // 4
ATTN_SCALE = HEAD_DIM ** -0.5
N_EXPERTS = 8
EPS = 1e-6

LANES = 128
N_GATES = 4 * B_HEADS
MASKED = -1e30

_OFF_AQ = 0
_OFF_AK = _OFF_AQ + A_Q_W
_OFF_AV = _OFF_AK + A_KV_W
_OFF_BQ = _OFF_AV + A_KV_W
_OFF_BK = _OFF_BQ + B_W
_OFF_BV = _OFF_BK + B_W
_OFF_BO = _OFF_BV + B_W
_OFF_CQ = _OFF_BO + B_W
_OFF_CK = _OFF_CQ + C_W
_OFF_CV = _OFF_CK + C_W
_OFF_BG = _OFF_CV + C_W
P_IN_PAD = _OFF_BG + LANES

TOKEN_TILE = 512
NA_Q_ROWS = 8
NA_K_ROWS = 16
VMEM_LIMIT = 56 * 1024 * 1024

F32 = jnp.float32
BF16 = jnp.bfloat16
HIGHEST = lax.Precision.HIGHEST
_NT = (((1,), (1,)), ((), ()))


def _params(*sem):
    return pltpu.CompilerParams(dimension_semantics=sem, vmem_limit_bytes=VMEM_LIMIT)


def _rms(x, g):
    return x * lax.rsqrt(jnp.mean(x * x, axis=-1, keepdims=True) + EPS) * g


def _silu(x):
    return x * jax.nn.sigmoid(x)


def _mod_kernel(c_ref, w_ref, b_ref, o_ref):
    s = _silu(c_ref[...])
    o_ref[0] = jnp.dot(s, w_ref[0], precision=HIGHEST, preferred_element_type=F32) + b_ref[0]


def _modulation(cvecs, w_mod, b_mod):
    nb = cvecs.shape[0]
    tn = 1536
    out = pl.pallas_call(
        _mod_kernel,
        grid=(DEPTH, 6 * D_MODEL // tn),
        in_specs=[pl.BlockSpec((nb, D_MODEL), lambda l, j: (0, 0)),
                  pl.BlockSpec((1, D_MODEL, tn), lambda l, j: (l, 0, j)),
                  pl.BlockSpec((1, 1, tn), lambda l, j: (l, 0, j))],
        out_specs=pl.BlockSpec((1, nb, tn), lambda l, j: (l, 0, j)),
        out_shape=jax.ShapeDtypeStruct((DEPTH, nb, 6 * D_MODEL), F32),
        compiler_params=_params("parallel", "parallel"),
        name="adaln_mod",
    )(cvecs, w_mod, b_mod.reshape(DEPTH, 1, 6 * D_MODEL))
    return out.reshape(DEPTH, nb, 6, D_MODEL)


def _inproj_kernel(*refs, rope, caches):
    x_ref, mod_ref, g_ref, w_ref, gq_ref, gk_ref = refs[:6]
    pos = 6
    if rope:
        cos_ref, sin_ref = refs[pos:pos + 2]
        pos += 2
    (qa_ref, ka_ref, va_ref, qb_ref, kb_ref, vb_ref, ob_ref, gb_ref,
     qc_ref, kc_ref, vc_ref) = refs[pos:pos + 11]
    pos += 11
    if caches:
        cka_ref, cva_ref, ckc_ref, cvc_ref = refs[pos:pos + 4]

    m = mod_ref[0]
    h = _rms(x_ref[...], g_ref[...]) * (1.0 + m[1:2]) + m[0:1]
    y = jnp.dot(h.astype(BF16), w_ref[...], preferred_element_type=F32)

    def head(off, j):
        return y[:, off + j * HEAD_DIM: off + (j + 1) * HEAD_DIM]

    def rotary(p):
        if not rope:
            return p
        swapped = jnp.concatenate([p[:, 16:32], p[:, 0:16], p[:, 48:64], p[:, 32:48]], axis=1)
        return p * cos_ref[...] + swapped * sin_ref[...]

    for j in range(A_HEADS):
        qa_ref[j] = (rotary(_rms(head(_OFF_AQ, j), gq_ref[...])) * ATTN_SCALE).astype(BF16)
    for j in range(A_KV_HEADS):
        kn = _rms(head(_OFF_AK, j), gk_ref[...])
        ka_ref[j] = rotary(kn).astype(BF16)
        va_ref[j] = head(_OFF_AV, j).astype(BF16)
        if caches:
            cka_ref[:, j * HEAD_DIM:(j + 1) * HEAD_DIM] = kn
    lane = lax.broadcasted_iota(jnp.int32, (y.shape[0], LANES), 1)
    ones_col = jnp.where(lane == HEAD_DIM, 1.0, 0.0)
    for j in range(B_HEADS):
        qb_ref[j] = (head(_OFF_BQ, j) * ATTN_SCALE).astype(BF16)
        kb_ref[j] = head(_OFF_BK, j).astype(BF16)
        wide = y[:, _OFF_BV + j * HEAD_DIM: _OFF_BV + j * HEAD_DIM + LANES]
        vb_ref[j] = jnp.where(lane < HEAD_DIM, wide, ones_col).astype(BF16)
    ob_ref[...] = y[:, _OFF_BO:_OFF_BO + B_W]
    gb_ref[...] = y[:, _OFF_BG:_OFF_BG + LANES]
    for j in range(C_HEADS):
        qc_ref[j] = (head(_OFF_CQ, j) * ATTN_SCALE).astype(BF16)
        kc_ref[j] = head(_OFF_CK, j).astype(BF16)
        vc_ref[j] = head(_OFF_CV, j).astype(BF16)
    if caches:
        cva_ref[...] = y[:, _OFF_AV:_OFF_AV + A_KV_W]
        ckc_ref[...] = y[:, _OFF_CK:_OFF_CK + C_W]
        cvc_ref[...] = y[:, _OFF_CV:_OFF_CV + C_W]


def _in_projection(x, mod, mod_row, g_pre, w_in, g_q, g_k, rope_tabs, caches):
    t = x.shape[0]
    tm = TOKEN_TILE
    rope = rope_tabs is not None
    row = lambda i: (i, 0)
    heads = lambda i: (0, i, 0)
    fixed2 = lambda i: (0, 0)
    in_specs = [pl.BlockSpec((tm, D_MODEL), row),
                pl.BlockSpec((1, 6, D_MODEL), lambda i: (mod_row(i), 0, 0)),
                pl.BlockSpec((1, D_MODEL), fixed2),
                pl.BlockSpec((D_MODEL, P_IN_PAD), fixed2),
                pl.BlockSpec((1, HEAD_DIM), fixed2),
                pl.BlockSpec((1, HEAD_DIM), fixed2)]
    args = [x, mod, g_pre, w_in, g_q, g_k]
    if rope:
        n_pos = rope_tabs[0].shape[0] // tm
        in_specs += [pl.BlockSpec((tm, HEAD_DIM), lambda i: (i % n_pos, 0))] * 2
        args += list(rope_tabs)

    def hm(nh, width=HEAD_DIM):
        return (jax.ShapeDtypeStruct((nh, t, width), BF16), pl.BlockSpec((nh, tm, width), heads))

    def tokmajor(width):
        return (jax.ShapeDtypeStruct((t, width), F32), pl.BlockSpec((tm, width), row))

    outs = [hm(A_HEADS), hm(A_KV_HEADS), hm(A_KV_HEADS),
            hm(B_HEADS), hm(B_HEADS), hm(B_HEADS, LANES), tokmajor(B_W), tokmajor(LANES),
            hm(C_HEADS), hm(C_HEADS), hm(C_HEADS)]
    if caches:
        outs += [tokmajor(A_KV_W), tokmajor(A_KV_W), tokmajor(C_W), tokmajor(C_W)]
    return pl.pallas_call(
        functools.partial(_inproj_kernel, rope=rope, caches=caches),
        grid=(t // tm,),
        in_specs=in_specs,
        out_specs=[o[1] for o in outs],
        out_shape=[o[0] for o in outs],
        compiler_params=_params("parallel"),
        name="in_projection",
    )(*args)


def _attn_kernel(*refs, ctx):
    if ctx:
        q_ref, k_ref, v_ref, kx_ref, vx_ref, o_ref = refs
    else:
        q_ref, k_ref, v_ref, o_ref = refs
    g, tq, d = q_ref.shape
    q = q_ref[...].reshape(g * tq, d)
    s = lax.dot_general(q, k_ref[0], _NT, preferred_element_type=F32)
    mx = jnp.max(s, axis=-1, keepdims=True)
    if ctx:
        sx = lax.dot_general(q, kx_ref[0], _NT, preferred_element_type=F32)
        mx = jnp.maximum(mx, jnp.max(sx, axis=-1, keepdims=True))
    p = jnp.exp(s - mx)
    den = jnp.sum(p, axis=-1, keepdims=True)
    o = jnp.dot(p.astype(BF16), v_ref[0], preferred_element_type=F32)
    if ctx:
        px = jnp.exp(sx - mx)
        den = den + jnp.sum(px, axis=-1, keepdims=True)
        o = o + jnp.dot(px.astype(BF16), vx_ref[0], preferred_element_type=F32)
    o_ref[...] = (o / den).reshape(g, tq, d)


def _attention(q, k, v, n_seq, tq, ctx=None):
    hq, t, d = q.shape
    hkv = k.shape[0]
    g = hq // hkv
    nb = t // n_seq
    nq = n_seq // tq
    in_specs = [pl.BlockSpec((g, tq, d), lambda b, kv, i: (kv, b * nq + i, 0)),
                pl.BlockSpec((1, n_seq, d), lambda b, kv, i: (kv, b, 0)),
                pl.BlockSpec((1, n_seq, d), lambda b, kv, i: (kv, b, 0))]
    args = [q, k, v]
    if ctx is not None:
        lc = ctx[0].shape[1] // nb
        in_specs += [pl.BlockSpec((1, lc, d), lambda b, kv, i: (kv, b, 0))] * 2
        args += list(ctx)
    return pl.pallas_call(
        functools.partial(_attn_kernel, ctx=ctx is not None),
        grid=(nb, hkv, nq),
        in_specs=in_specs,
        out_specs=pl.BlockSpec((g, tq, d), lambda b, kv, i: (kv, b * nq + i, 0)),
        out_shape=jax.ShapeDtypeStruct((hq, t, d), F32),
        compiler_params=_params("parallel", "parallel", "parallel"),
        name="dense_attention",
    )(*args)


def _na_kernel(q_ref, k_ref, v_ref, kx_ref, vx_ref, toep_ref, o_ref, bias_scr):
    i = pl.program_id(1)
    nk = bias_scr.shape[1]
    rows = k_ref.shape[1] // GRID_W
    first_row = jnp.clip(NA_Q_ROWS * i - NA_WIN_R // 2, 0, rows - NA_K_ROWS)

    @pl.when(pl.program_id(2) == 0)
    def _():
        for a in range(NA_Q_ROWS):
            qr = NA_Q_ROWS * i + a
            r_start = jnp.clip(qr - NA_WIN_R // 2, 0, rows - NA_WIN_R)
            for w in range(NA_K_ROWS):
                kr = first_row + w
                in_window = (kr >= r_start) & (kr < r_start + NA_WIN_R)
                dr = jnp.clip(kr - qr + NA_WIN_R - 1, 0, 2 * NA_WIN_R - 2)
                bias_scr[a * GRID_W:(a + 1) * GRID_W, w * GRID_W:(w + 1) * GRID_W] = (
                    toep_ref[0, dr] + jnp.where(in_window, 0.0, MASKED))

    start = pl.multiple_of(first_row * GRID_W, GRID_W)
    q = q_ref[0]
    k = k_ref[0, pl.ds(start, nk), :]
    v = v_ref[0, pl.ds(start, nk), :]
    s = lax.dot_general(q, k, _NT, preferred_element_type=F32) + bias_scr[...]
    sx = lax.dot_general(q, kx_ref[0], _NT, preferred_element_type=F32)
    mx = jnp.maximum(jnp.max(s, axis=-1, keepdims=True), jnp.max(sx, axis=-1, keepdims=True))
    p = jnp.exp(s - mx)
    px = jnp.exp(sx - mx)
    den = jnp.sum(p, axis=-1, keepdims=True) + jnp.sum(px, axis=-1, keepdims=True)
    o = (jnp.dot(p.astype(BF16), v, preferred_element_type=F32)
         + jnp.dot(px.astype(BF16), vx_ref[0], preferred_element_type=F32))
    o_ref[0] = o / den


def _na_toeplitz(rpb):
    qc = jnp.arange(GRID_W, dtype=jnp.int32)[:, None]
    kc = jnp.arange(GRID_W, dtype=jnp.int32)[None, :]
    c_start = jnp.clip(qc - NA_WIN_C // 2, 0, GRID_W - NA_WIN_C)
    in_window = (kc >= c_start) & (kc < c_start + NA_WIN_C)
    offs = jnp.arange(2 * NA_WIN_C - 1, dtype=jnp.int32)[:, None, None]
    onehot = ((kc - qc + NA_WIN_C - 1)[None] == offs) & in_window[None]
    table = jnp.einsum('hrb,bqk->hrqk', rpb.astype(F32), onehot.astype(F32), precision=HIGHEST)
    return table + jnp.where(in_window, 0.0, MASKED)


def _neighbourhood_attention(q, k, v, kx, vx, toep, n_seq):
    h, t, d = q.shape
    nb = t // n_seq
    tq = NA_Q_ROWS * GRID_W
    nblk = n_seq // tq
    lc = kx.shape[1] // nb
    return pl.pallas_call(
        _na_kernel,
        grid=(h, nblk, nb),
        in_specs=[pl.BlockSpec((1, tq, d), lambda hh, i, b: (hh, b * nblk + i, 0)),
                  pl.BlockSpec((1, n_seq, d), lambda hh, i, b: (hh, b, 0)),
                  pl.BlockSpec((1, n_seq, d), lambda hh, i, b: (hh, b, 0)),
                  pl.BlockSpec((1, lc, d), lambda hh, i, b: (hh, b, 0)),
                  pl.BlockSpec((1, lc, d), lambda hh, i, b: (hh, b, 0)),
                  pl.BlockSpec((1,) + toep.shape[1:], lambda hh, i, b: (hh, 0, 0, 0))],
        out_specs=pl.BlockSpec((1, tq, d), lambda hh, i, b: (hh, b * nblk + i, 0)),
        out_shape=jax.ShapeDtypeStruct((h, t, d), F32),
        scratch_shapes=[pltpu.VMEM((tq, NA_K_ROWS * GRID_W), F32)],
        compiler_params=_params("parallel", "parallel", "arbitrary"),
        name="neighbourhood_attention",
    )(q, k, v, kx, vx, toep)


def _mlstm_kernel(q_ref, k_ref, v_ref, o_ref, g_ref, gbias_ref, gout_ref, c0_ref, m0_ref,
                  h_ref, cn_ref, mn_ref, hs_ref, c_scr, m_scr):
    L = MLSTM_CHUNK
    n = q_ref.shape[1]
    nc = n // L
    hs_ref[...] = jnp.zeros(hs_ref.shape, F32)
    c_scr[...] = c0_ref[0]
    m_scr[...] = m0_ref[0]

    row = lax.broadcasted_iota(jnp.int32, (L, L), 0)
    col = lax.broadcasted_iota(jnp.int32, (L, L), 1)
    masks = (col <= row, col >= row)
    cum_mats = tuple(mk.astype(F32) for mk in masks)
    lane = lax.broadcasted_iota(jnp.int32, (L, LANES), 1)
    eye_l = (lax.broadcasted_iota(jnp.int32, (LANES, LANES), 0)
             == lax.broadcasted_iota(jnp.int32, (LANES, LANES), 1)).astype(F32)
    eye_k = (row == col).astype(BF16)

    def column(xmat, j):
        return jnp.sum(jnp.where(lane == j, xmat, 0.0), axis=1, keepdims=True)

    def body(i, carry):
        for d in range(2):
            c = i if d == 0 else nc - 1 - i
            rows = pl.ds(pl.multiple_of(c * L, L), L)
            pre = g_ref[rows, :] + gbias_ref[...]
            logf = jnp.minimum(pre, 0.0) - jnp.log1p(jnp.exp(-jnp.abs(pre)))
            cum = jnp.dot(cum_mats[d], logf, precision=HIGHEST, preferred_element_type=F32)
            pre_t = lax.dot_general(eye_l, pre, _NT, precision=HIGHEST, preferred_element_type=F32)
            cum_t = lax.dot_general(eye_l, cum, _NT, precision=HIGHEST, preferred_element_type=F32)
            last = L - 1 if d == 0 else 0
            for hh in range(B_HEADS):
                s_idx = d * B_HEADS + hh
                ji = d * 2 * B_HEADS + hh
                jf = ji + B_HEADS
                b_col = column(cum, jf)
                li_col = column(pre, ji)
                b_row = cum_t[jf:jf + 1, :]
                li_row = pre_t[ji:ji + 1, :]
                m_prev = m_scr[s_idx:s_idx + 1, 0:1]
                c_aug = c_scr[s_idx]
                log_d = jnp.where(masks[d], b_col - b_row + li_row, -jnp.inf)
                m_inter = b_col + m_prev
                m_t = jnp.maximum(m_inter, jnp.max(log_d, axis=-1, keepdims=True))
                w_inter = jnp.exp(m_inter - m_t)
                qc = q_ref[hh, rows, :]
                kc = k_ref[hh, rows, :]
                vc = v_ref[hh, rows, :]
                s = lax.dot_general(qc, kc, _NT, preferred_element_type=F32) * jnp.exp(log_d - m_t)
                num = (jnp.dot(s.astype(BF16), vc, preferred_element_type=F32)
                       + w_inter * jnp.dot(qc, c_aug.astype(BF16), preferred_element_type=F32))
                den = column(num, HEAD_DIM)
                hs_ref[hh, rows, :] += num[:, :HEAD_DIM] / jnp.maximum(jnp.abs(den), jnp.exp(-m_t))
                m_new = m_t[last:last + 1, :]
                b_last = b_col[last:last + 1, :]
                w = jnp.exp(b_last - b_col + li_col - m_new)
                decay = jnp.exp(b_last + m_prev - m_new)
                k_t = lax.dot_general(eye_k, kc, _NT, preferred_element_type=F32).astype(BF16)
                wv = (w * vc.astype(F32)).astype(BF16)
                c_scr[s_idx] = decay * c_aug + jnp.dot(k_t, wv, preferred_element_type=F32)
                m_scr[s_idx:s_idx + 1, :] = jnp.broadcast_to(m_new, (1, LANES))
        return carry

    lax.fori_loop(0, nc, body, 0)

    for hh in range(B_HEADS):
        gate = jax.nn.sigmoid(o_ref[:, hh * HEAD_DIM:(hh + 1) * HEAD_DIM])
        h_ref[hh] = _rms(hs_ref[hh], gout_ref[hh]) * gate
    cn_ref[0] = c_scr[...]
    mn_ref[0] = m_scr[...]


def _mlstm(q, k, v, o, g, gate_bias, g_out, c0, m0, n_seq):
    h, t, d = q.shape
    nb = t // n_seq
    ns = 2 * h
    heads = lambda b: (0, b, 0)
    return pl.pallas_call(
        _mlstm_kernel,
        grid=(nb,),
        in_specs=[pl.BlockSpec((h, n_seq, d), heads),
                  pl.BlockSpec((h, n_seq, d), heads),
                  pl.BlockSpec((h, n_seq, LANES), heads),
                  pl.BlockSpec((n_seq, h * d), lambda b: (b, 0)),
                  pl.BlockSpec((n_seq, LANES), lambda b: (b, 0)),
                  pl.BlockSpec((1, LANES), lambda b: (0, 0)),
                  pl.BlockSpec((h, 1, d), lambda b: (0, 0, 0)),
                  pl.BlockSpec((1, ns, d, LANES), lambda b: (b, 0, 0, 0)),
                  pl.BlockSpec((1, ns, LANES), lambda b: (b, 0, 0))],
        out_specs=[pl.BlockSpec((h, n_seq, d), heads),
                   pl.BlockSpec((1, ns, d, LANES), lambda b: (b, 0, 0, 0)),
                   pl.BlockSpec((1, ns, LANES), lambda b: (b, 0, 0))],
        out_shape=[jax.ShapeDtypeStruct((h, t, d), F32),
                   jax.ShapeDtypeStruct((nb, ns, d, LANES), F32),
                   jax.ShapeDtypeStruct((nb, ns, LANES), F32)],
        scratch_shapes=[pltpu.VMEM((h, n_seq, d), F32),
                        pltpu.VMEM((ns, d, LANES), F32),
                        pltpu.VMEM((ns, LANES), F32)],
        compiler_params=_params("parallel"),
        name="mlstm",
    )(q, k, v, o, g, gate_bias, g_out, c0, m0)


def _outproj_kernel(*refs, moe):
    a_ref, b_ref, c_ref, w_ref, x_ref, mod_ref, gpost_ref, gpre_ref = refs[:8]
    pos = 8
    if moe:
        wr_ref = refs[pos]
        pos += 1
    xo_ref, h2_ref = refs[pos:pos + 2]
    pos += 2
    if moe:
        gates_ref = refs[pos]
        pos += 1
    cat_ref = refs[pos]

    off = 0
    for src in (a_ref, b_ref, c_ref):
        for j in range(src.shape[0]):
            cat_ref[:, off:off + HEAD_DIM] = src[j]
            off += HEAD_DIM
    out = jnp.dot(cat_ref[...].astype(BF16), w_ref[...], preferred_element_type=F32)
    m = mod_ref[0]
    xn = x_ref[...] + m[2:3] * _rms(out, gpost_ref[...])
    xo_ref[...] = xn
    h2 = _rms(xn, gpre_ref[...]) * (1.0 + m[4:5]) + m[3:4]
    h2_ref[...] = h2.astype(BF16)
    if moe:
        logits = jnp.dot(h2, wr_ref[...], precision=HIGHEST, preferred_element_type=F32)
        lane = lax.broadcasted_iota(jnp.int32, logits.shape, 1)
        logits = jnp.where(lane < N_EXPERTS, logits, -jnp.inf)
        v1 = jnp.max(logits, axis=-1, keepdims=True)
        i1 = jnp.min(jnp.where(logits == v1, lane, LANES), axis=-1, keepdims=True)
        rest = jnp.where(lane == i1, -jnp.inf, logits)
        v2 = jnp.max(rest, axis=-1, keepdims=True)
        i2 = jnp.min(jnp.where(rest == v2, lane, LANES), axis=-1, keepdims=True)
        e2 = jnp.exp(v2 - v1)
        w1 = 1.0 / (1.0 + e2)
        w2 = e2 / (1.0 + e2)
        gates_ref[...] = jnp.where(lane == i1, w1, 0.0) + jnp.where(lane == i2, w2, 0.0)


def _out_projection(a, b, c, w_out, x, mod, mod_row, g_post, g_pre, w_router):
    t = x.shape[0]
    tm = TOKEN_TILE
    moe = w_router is not None
    row = lambda i: (i, 0)
    fixed2 = lambda i: (0, 0)
    heads = lambda i: (0, i, 0)
    in_specs = [pl.BlockSpec((a.shape[0], tm, HEAD_DIM), heads),
                pl.BlockSpec((b.shape[0], tm, HEAD_DIM), heads),
                pl.BlockSpec((c.shape[0], tm, HEAD_DIM), heads),
                pl.BlockSpec((D_MODEL, D_MODEL), fixed2),
                pl.BlockSpec((tm, D_MODEL), row),
                pl.BlockSpec((1, 6, D_MODEL), lambda i: (mod_row(i), 0, 0)),
                pl.BlockSpec((1, D_MODEL), fixed2),
                pl.BlockSpec((1, D_MODEL), fixed2)]
    args = [a, b, c, w_out, x, mod, g_post, g_pre]
    out_specs = [pl.BlockSpec((tm, D_MODEL), row), pl.BlockSpec((tm, D_MODEL), row)]
    out_shape = [jax.ShapeDtypeStruct((t, D_MODEL), F32), jax.ShapeDtypeStruct((t, D_MODEL), BF16)]
    if moe:
        in_specs.append(pl.BlockSpec((D_MODEL, LANES), fixed2))
        args.append(w_router)
        out_specs.append(pl.BlockSpec((tm, LANES), row))
        out_shape.append(jax.ShapeDtypeStruct((t, LANES), F32))
    return pl.pallas_call(
        functools.partial(_outproj_kernel, moe=moe),
        grid=(t // tm,),
        in_specs=in_specs,
        out_specs=out_specs,
        out_shape=out_shape,
        scratch_shapes=[pltpu.VMEM((tm, D_MODEL), F32)],
        compiler_params=_params("parallel"),
        name="out_projection",
    )(*args)


def _swiglu_chunk(h, wg, wu, wd):
    a = jnp.dot(h, wg, preferred_element_type=F32)
    b = jnp.dot(h, wu, preferred_element_type=F32)
    return jnp.dot((_silu(a) * b).astype(BF16), wd, preferred_element_type=F32)


def _ffn_kernel(h_ref, wg_ref, wu_ref, wd_ref, x_ref, mod_ref, g_ref, o_ref, acc_ref):
    j = pl.program_id(1)

    @pl.when(j == 0)
    def _():
        acc_ref[...] = jnp.zeros(acc_ref.shape, F32)

    acc_ref[...] += _swiglu_chunk(h_ref[...], wg_ref[...], wu_ref[...], wd_ref[...])

    @pl.when(j == pl.num_programs(1) - 1)
    def _():
        o_ref[...] = x_ref[...] + mod_ref[0][5:6] * _rms(acc_ref[...], g_ref[...])


def _dense_ffn(h2, wg, wu, wd, x, mod, mod_row, g_post):
    t = x.shape[0]
    tm = TOKEN_TILE
    ff = wg.shape[1]
    tf = ff // 2
    return pl.pallas_call(
        _ffn_kernel,
        grid=(t // tm, ff // tf),
        in_specs=[pl.BlockSpec((tm, D_MODEL), lambda i, j: (i, 0)),
                  pl.BlockSpec((D_MODEL, tf), lambda i, j: (0, j)),
                  pl.BlockSpec((D_MODEL, tf), lambda i, j: (0, j)),
                  pl.BlockSpec((tf, D_MODEL), lambda i, j: (j, 0)),
                  pl.BlockSpec((tm, D_MODEL), lambda i, j: (i, 0)),
                  pl.BlockSpec((1, 6, D_MODEL), lambda i, j: (mod_row(i), 0, 0)),
                  pl.BlockSpec((1, D_MODEL), lambda i, j: (0, 0))],
        out_specs=pl.BlockSpec((tm, D_MODEL), lambda i, j: (i, 0)),
        out_shape=jax.ShapeDtypeStruct((t, D_MODEL), F32),
        scratch_shapes=[pltpu.VMEM((tm, D_MODEL), F32)],
        compiler_params=_params("parallel", "arbitrary"),
        name="dense_ffn",
    )(h2, wg, wu, wd, x, mod, g_post)


def _moe_kernel(h_ref, gates_ref, wg_ref, wu_ref, wd_ref, x_ref, mod_ref, g_ref, o_ref, acc_ref):
    e = pl.program_id(1)
    j = pl.program_id(2)

    @pl.when((e == 0) & (j == 0))
    def _():
        acc_ref[...] = jnp.zeros(acc_ref.shape, F32)

    gates = gates_ref[...]
    lane = lax.broadcasted_iota(jnp.int32, gates.shape, 1)
    gate = jnp.sum(jnp.where(lane == e, gates, 0.0), axis=-1, keepdims=True)
    acc_ref[...] += gate * _swiglu_chunk(h_ref[...], wg_ref[0], wu_ref[0], wd_ref[0])

    @pl.when((e == pl.num_programs(1) - 1) & (j == pl.num_programs(2) - 1))
    def _():
        o_ref[...] = x_ref[...] + mod_ref[0][5:6] * _rms(acc_ref[...], g_ref[...])


def _moe_ffn(h2, gates, wg, wu, wd, x, mod, mod_row, g_post):
    t = x.shape[0]
    tm = 2 * TOKEN_TILE
    ne, _, ff = wg.shape
    tf = 512
    mrow = lambda i: mod_row(i * (tm // TOKEN_TILE))
    return pl.pallas_call(
        _moe_kernel,
        grid=(t // tm, ne, ff // tf),
        in_specs=[pl.BlockSpec((tm, D_MODEL), lambda i, e, j: (i, 0)),
                  pl.BlockSpec((tm, LANES), lambda i, e, j: (i, 0)),
                  pl.BlockSpec((1, D_MODEL, tf), lambda i, e, j: (e, 0, j)),
                  pl.BlockSpec((1, D_MODEL, tf), lambda i, e, j: (e, 0, j)),
                  pl.BlockSpec((1, tf, D_MODEL), lambda i, e, j: (e, j, 0)),
                  pl.BlockSpec((tm, D_MODEL), lambda i, e, j: (i, 0)),
                  pl.BlockSpec((1, 6, D_MODEL), lambda i, e, j: (mrow(i), 0, 0)),
                  pl.BlockSpec((1, D_MODEL), lambda i, e, j: (0, 0))],
        out_specs=pl.BlockSpec((tm, D_MODEL), lambda i, e, j: (i, 0)),
        out_shape=jax.ShapeDtypeStruct((t, D_MODEL), F32),
        scratch_shapes=[pltpu.VMEM((tm, D_MODEL), F32)],
        compiler_params=_params("parallel", "arbitrary", "arbitrary"),
        name="moe_ffn",
    )(h2, gates, wg, wu, wd, x, mod, g_post)


def _reorder_w_in(w):
    gate0 = _OFF_BO + B_W
    body = jnp.concatenate([w[:, :gate0], w[:, gate0 + N_GATES:]], axis=1)
    gates = w[:, gate0:gate0 + N_GATES]
    pad = jnp.zeros((w.shape[0], LANES - N_GATES), w.dtype)
    return jnp.concatenate([body, gates, pad], axis=1).astype(BF16)


def _rope_tables(n):
    tok = jnp.arange(n, dtype=jnp.int32)
    pos = jnp.stack([tok // GRID_W, tok % GRID_W], axis=-1).astype(F32)
    inv = ROPE_THETA ** (-jnp.arange(ROPE_QUARTER, dtype=F32) / ROPE_QUARTER)
    ang = pos[:, :, None] * inv
    cos, sin = jnp.cos(ang), jnp.sin(ang)
    cos_t = jnp.concatenate([cos[:, 0], cos[:, 0], cos[:, 1], cos[:, 1]], axis=-1)
    sin_t = jnp.concatenate([-sin[:, 0], sin[:, 0], -sin[:, 1], sin[:, 1]], axis=-1)
    return cos_t, sin_t


def _head_major(cache):
    b, l, h, d = cache.shape
    return jnp.transpose(cache, (2, 0, 1, 3)).reshape(h, b * l, d).astype(BF16)


def kernel(x_prompt, x_sample, cache_gqa_k, cache_gqa_v, cache_na_k, cache_na_v, state_mlstm_C, state_mlstm_n, state_mlstm_m, c, c_ctx, w_mod, b_mod, g_pre_mix, g_post_mix, g_pre_ffn, g_post_ffn, w_in, w_out, g_q, g_k, mlstm_gate_bias, g_mlstm_out, na_rpb, w_ffn_gate, w_ffn_up, w_ffn_down, w_router, w_exp_gate, w_exp_up, w_exp_down):
    batch, seq, _ = x_prompt.shape
    dec_batch, dec_seq, _ = x_sample.shape
    assert seq % TOKEN_TILE == 0 or TOKEN_TILE % seq == 0
    assert dec_seq % (2 * TOKEN_TILE) == 0 and dec_seq % (NA_Q_ROWS * GRID_W) == 0
    tp, ts = batch * seq, dec_batch * dec_seq
    xp = x_prompt.reshape(tp, D_MODEL)
    xs = x_sample.reshape(ts, D_MODEL)

    n_mod = 8
    cvecs = jnp.concatenate([c_ctx[None, :], c, jnp.zeros((n_mod - 1 - dec_batch, D_MODEL), F32)], axis=0)
    mod_all = _modulation(cvecs, w_mod, b_mod)
    prompt_row = lambda i: 0
    tiles_per_sample = dec_seq // TOKEN_TILE
    sample_row = lambda i: 1 + i // tiles_per_sample

    rope_tabs = _rope_tables(dec_seq)
    row2 = lambda v: v.reshape(1, -1)
    ns = 2 * B_HEADS
    zeros_c = jnp.zeros((batch, ns, HEAD_DIM, LANES), F32)
    zeros_m = jnp.zeros((batch, ns, LANES), F32)

    ka_l, va_l, kc_l, vc_l, cst_l, m_l = [], [], [], [], [], []
    for l in range(DEPTH):
        mod = mod_all[l]
        w_in_l = _reorder_w_in(w_in[l])
        w_out_l = w_out[l].astype(BF16)
        gbias = jnp.concatenate([mlstm_gate_bias[l].reshape(1, N_GATES),
                                 jnp.zeros((1, LANES - N_GATES), F32)], axis=1)
        gout = g_mlstm_out[l].reshape(B_HEADS, 1, HEAD_DIM)
        moe = l % 2 == 1
        if moe:
            w_r = jnp.concatenate([w_router[l // 2], jnp.zeros((D_MODEL, LANES - N_EXPERTS), F32)], axis=1)
            ffn_w = tuple(w[l // 2].astype(BF16) for w in (w_exp_gate, w_exp_up, w_exp_down))
        else:
            w_r = None
            ffn_w = tuple(w[l // 2].astype(BF16) for w in (w_ffn_gate, w_ffn_up, w_ffn_down))

        def channel_mixer(res, x_mid, mod_row):
            if moe:
                return _moe_ffn(res[1], res[2], *ffn_w, x_mid, mod, mod_row, row2(g_post_ffn[l]))
            return _dense_ffn(res[1], *ffn_w, x_mid, mod, mod_row, row2(g_post_ffn[l]))

        (qa, ka, va, qb, kb, vb, ob, gb, qc, kc, vc, cka, cva, ckc, cvc) = _in_projection(
            xp, mod, prompt_row, row2(g_pre_mix[l]), w_in_l, row2(g_q[l]), row2(g_k[l]), None, True)
        a_out = _attention(qa, ka, va, seq, seq)
        b_out, cst, mst = _mlstm(qb, kb, vb, ob, gb, gbias, gout, zeros_c, zeros_m, seq)
        c_out = _attention(qc, kc, vc, seq, seq)
        res = _out_projection(a_out, b_out, c_out, w_out_l, xp, mod, prompt_row,
                              row2(g_post_mix[l]), row2(g_pre_ffn[l]), w_r)
        xp = channel_mixer(res, res[0], prompt_row)
        ka_l.append(cka.reshape(batch, seq, A_KV_HEADS, HEAD_DIM))
        va_l.append(cva.reshape(batch, seq, A_KV_HEADS, HEAD_DIM))
        kc_l.append(ckc.reshape(batch, seq, C_HEADS, HEAD_DIM))
        vc_l.append(cvc.reshape(batch, seq, C_HEADS, HEAD_DIM))
        cst_l.append(cst.reshape(batch, 2, B_HEADS, HEAD_DIM, LANES))
        m_l.append(mst.reshape(batch, 2, B_HEADS, LANES)[..., 0])

        (qa, ka, va, qb, kb, vb, ob, gb, qc, kc, vc) = _in_projection(
            xs, mod, sample_row, row2(g_pre_mix[l]), w_in_l, row2(g_q[l]), row2(g_k[l]), rope_tabs, False)
        a_out = _attention(qa, ka, va, dec_seq, 256,
                           ctx=(_head_major(cache_gqa_k[:, l]), _head_major(cache_gqa_v[:, l])))
        c0 = jnp.concatenate([state_mlstm_C[:, l], state_mlstm_n[:, l][..., None],
                              jnp.zeros((dec_batch, 2, B_HEADS, HEAD_DIM, LANES - HEAD_DIM - 1), F32)],
                             axis=-1).reshape(dec_batch, ns, HEAD_DIM, LANES)
        m0 = jnp.broadcast_to(state_mlstm_m[:, l].reshape(dec_batch, ns, 1), (dec_batch, ns, LANES))
        b_out, _, _ = _mlstm(qb, kb, vb, ob, gb, gbias, gout, c0, m0, dec_seq)
        c_out = _neighbourhood_attention(qc, kc, vc, _head_major(cache_na_k[:, l]),
                                         _head_major(cache_na_v[:, l]),
                                         _na_toeplitz(na_rpb[l]), dec_seq)
        res = _out_projection(a_out, b_out, c_out, w_out_l, xs, mod, sample_row,
                              row2(g_post_mix[l]), row2(g_pre_ffn[l]), w_r)
        xs = channel_mixer(res, res[0], sample_row)

    cst = jnp.stack(cst_l, axis=1)
    return (xp.reshape(batch, seq, D_MODEL), xs.reshape(dec_batch, dec_seq, D_MODEL),
            jnp.stack(ka_l, axis=1), jnp.stack(va_l, axis=1),
            jnp.stack(kc_l, axis=1), jnp.stack(vc_l, axis=1),
            cst[..., :HEAD_DIM], cst[..., HEAD_DIM], jnp.stack(m_l, axis=1))
```

```python
import functools

import jax
import jax.numpy as jnp
from jax import lax
from jax.experimental import pallas as pl
from jax.experimental.pallas import tpu as pltpu

D_MODEL = 1024
DEPTH = 2
GRID_W = 64
HEAD_DIM = 64
A_HEADS = 6
A_KV_HEADS = 2
A_GROUP = A_HEADS // A_KV_HEADS
B_HEADS = 4
C_HEADS = 6
A_Q_W = A_HEADS * HEAD_DIM
A_KV_W = A_KV_HEADS * HEAD_DIM
B_W = B_HEADS * HEAD_DIM
C_W = C_HEADS * HEAD_DIM
MLSTM_CHUNK = 64
NA_WIN_R = 8
NA_WIN_C = 16
ROPE_THETA = 10000.0
ROPE_QUARTER = HEAD_DIM // 4
ATTN_SCALE = HEAD_DIM ** -0.5
N_EXPERTS = 8
EPS = 1e-6

LANES = 128
N_GATES = 4 * B_HEADS
MASKED = -1e30

_OFF_AQ = 0
_OFF_AK = _OFF_AQ + A_Q_W
_OFF_AV = _OFF_AK + A_KV_W
_OFF_BQ = _OFF_AV + A_KV_W
_OFF_BK = _OFF_BQ + B_W
_OFF_BV = _OFF_BK + B_W
_OFF_BO = _OFF_BV + B_W
_OFF_CQ = _OFF_BO + B_W
_OFF_CK = _OFF_CQ + C_W
_OFF_CV = _OFF_CK + C_W
_OFF_BG = _OFF_CV + C_W
P_IN_PAD = _OFF_BG + LANES

TOKEN_TILE = 512
NA_Q_ROWS = 8
NA_K_ROWS = 16
VMEM_LIMIT = 56 * 1024 * 1024

F32 = jnp.float32
BF16 = jnp.bfloat16
HIGHEST = lax.Precision.HIGHEST
_NT = (((1,), (1,)), ((), ()))


def _params(*sem):
    return pltpu.CompilerParams(dimension_semantics=sem, vmem_limit_bytes=VMEM_LIMIT)


def _rms(x, g):
    return x * lax.rsqrt(jnp.mean(x * x, axis=-1, keepdims=True) + EPS) * g


def _silu(x):
    return x * jax.nn.sigmoid(x)


def _mod_kernel(c_ref, w_ref, b_ref, o_ref):
    s = _silu(c_ref[...])
    o_ref[0] = jnp.dot(s, w_ref[0], precision=HIGHEST, preferred_element_type=F32) + b_ref[0]


def _modulation(cvecs, w_mod, b_mod):
    nb = cvecs.shape[0]
    tn = 1536
    out = pl.pallas_call(
        _mod_kernel,
        grid=(DEPTH, 6 * D_MODEL // tn),
        in_specs=[pl.BlockSpec((nb, D_MODEL), lambda l, j: (0, 0)),
                  pl.BlockSpec((1, D_MODEL, tn), lambda l, j: (l, 0, j)),
                  pl.BlockSpec((1, 1, tn), lambda l, j: (l, 0, j))],
        out_specs=pl.BlockSpec((1, nb, tn), lambda l, j: (l, 0, j)),
        out_shape=jax.ShapeDtypeStruct((DEPTH, nb, 6 * D_MODEL), F32),
        compiler_params=_params("parallel", "parallel"),
        name="adaln_mod",
    )(cvecs, w_mod, b_mod.reshape(DEPTH, 1, 6 * D_MODEL))
    return out.reshape(DEPTH, nb, 6, D_MODEL)


def _inproj_kernel(*refs, rope, caches):
    x_ref, mod_ref, g_ref, w_ref, gq_ref, gk_ref = refs[:6]
    pos = 6
    if rope:
        cos_ref, sin_ref = refs[pos:pos + 2]
        pos += 2
    (qa_ref, ka_ref, va_ref, qb_ref, kb_ref, vb_ref, ob_ref, gb_ref,
     qc_ref, kc_ref, vc_ref) = refs[pos:pos + 11]
    pos += 11
    if caches:
        cka_ref, cva_ref, ckc_ref, cvc_ref = refs[pos:pos + 4]

    m = mod_ref[0]
    h = _rms(x_ref[...], g_ref[...]) * (1.0 + m[1:2]) + m[0:1]
    y = jnp.dot(h.astype(BF16), w_ref[...], preferred_element_type=F32)

    def head(off, j):
        return y[:, off + j * HEAD_DIM: off + (j + 1) * HEAD_DIM]

    def rotary(p):
        if not rope:
            return p
        swapped = jnp.concatenate([p[:, 16:32], p[:, 0:16], p[:, 48:64], p[:, 32:48]], axis=1)
        return p * cos_ref[...] + swapped * sin_ref[...]

    for j in range(A_HEADS):
        qa_ref[j] = (rotary(_rms(head(_OFF_AQ, j), gq_ref[...])) * ATTN_SCALE).astype(BF16)
    for j in range(A_KV_HEADS):
        kn = _rms(head(_OFF_AK, j), gk_ref[...])
        ka_ref[j] = rotary(kn).astype(BF16)
        va_ref[j] = head(_OFF_AV, j).astype(BF16)
        if caches:
            cka_ref[:, j * HEAD_DIM:(j + 1) * HEAD_DIM] = kn
    lane = lax.broadcasted_iota(jnp.int32, (y.shape[0], LANES), 1)
    ones_col = jnp.where(lane == HEAD_DIM, 1.0, 0.0)
    for j in range(B_HEADS):
        qb_ref[j] = (head(_OFF_BQ, j) * ATTN_SCALE).astype(BF16)
        kb_ref[j] = head(_OFF_BK, j).astype(BF16)
        wide = y[:, _OFF_BV + j * HEAD_DIM: _OFF_BV + j * HEAD_DIM + LANES]
        vb_ref[j] = jnp.where(lane < HEAD_DIM, wide, ones_col).astype(BF16)
    ob_ref[...] = y[:, _OFF_BO:_OFF_BO + B_W]
    gb_ref[...] = y[:, _OFF_BG:_OFF_BG + LANES]
    for j in range(C_HEADS):
        qc_ref[j] = (head(_OFF_CQ, j) * ATTN_SCALE).astype(BF16)
        kc_ref[j] = head(_OFF_CK, j).astype(BF16)
        vc_ref[j] = head(_OFF_CV, j).astype(BF16)
    if caches:
        cva_ref[...] = y[:, _OFF_AV:_OFF_AV + A_KV_W]
        ckc_ref[...] = y[:, _OFF_CK:_OFF_CK + C_W]
        cvc_ref[...] = y[:, _OFF_CV:_OFF_CV + C_W]


def _in_projection(x, mod, mod_row, g_pre, w_in, g_q, g_k, rope_tabs, caches):
    t = x.shape[0]
    tm = TOKEN_TILE
    rope = rope_tabs is not None
    row = lambda i: (i, 0)
    heads = lambda i: (0, i, 0)
    fixed2 = lambda i: (0, 0)
    in_specs = [pl.BlockSpec((tm, D_MODEL), row),
                pl.BlockSpec((1, 6, D_MODEL), lambda i: (mod_row(i), 0, 0)),
                pl.BlockSpec((1, D_MODEL), fixed2),
                pl.BlockSpec((D_MODEL, P_IN_PAD), fixed2),
                pl.BlockSpec((1, HEAD_DIM), fixed2),
                pl.BlockSpec((1, HEAD_DIM), fixed2)]
    args = [x, mod, g_pre, w_in, g_q, g_k]
    if rope:
        n_pos = rope_tabs[0].shape[0] // tm
        in_specs += [pl.BlockSpec((tm, HEAD_DIM), lambda i: (i % n_pos, 0))] * 2
        args += list(rope_tabs)

    def hm(nh, width=HEAD_DIM):
        return (jax.ShapeDtypeStruct((nh, t, width), BF16), pl.BlockSpec((nh, tm, width), heads))

    def tokmajor(width):
        return (jax.ShapeDtypeStruct((t, width), F32), pl.BlockSpec((tm, width), row))

    outs = [hm(A_HEADS), hm(A_KV_HEADS), hm(A_KV_HEADS),
            hm(B_HEADS), hm(B_HEADS), hm(B_HEADS, LANES), tokmajor(B_W), tokmajor(LANES),
            hm(C_HEADS), hm(C_HEADS), hm(C_HEADS)]
    if caches:
        outs += [tokmajor(A_KV_W), tokmajor(A_KV_W), tokmajor(C_W), tokmajor(C_W)]
    return pl.pallas_call(
        functools.partial(_inproj_kernel, rope=rope, caches=caches),
        grid=(t // tm,),
        in_specs=in_specs,
        out_specs=[o[1] for o in outs],
        out_shape=[o[0] for o in outs],
        compiler_params=_params("parallel"),
        name="in_projection",
    )(*args)


def _attn_kernel(*refs, ctx):
    if ctx:
        q_ref, k_ref, v_ref, kx_ref, vx_ref, o_ref = refs
    else:
        q_ref, k_ref, v_ref, o_ref = refs
    g, tq, d = q_ref.shape
    q = q_ref[...].reshape(g * tq, d)
    s = lax.dot_general(q, k_ref[0], _NT, preferred_element_type=F32)
    mx = jnp.max(s, axis=-1, keepdims=True)
    if ctx:
        sx = lax.dot_general(q, kx_ref[0], _NT, preferred_element_type=F32)
        mx = jnp.maximum(mx, jnp.max(sx, axis=-1, keepdims=True))
    p = jnp.exp(s - mx)
    den = jnp.sum(p, axis=-1, keepdims=True)
    o = jnp.dot(p.astype(BF16), v_ref[0], preferred_element_type=F32)
    if ctx:
        px = jnp.exp(sx - mx)
        den = den + jnp.sum(px, axis=-1, keepdims=True)
        o = o + jnp.dot(px.astype(BF16), vx_ref[0], preferred_element_type=F32)
    o_ref[...] = (o / den).reshape(g, tq, d)


def _attention(q, k, v, n_seq, tq, ctx=None):
    hq, t, d = q.shape
    hkv = k.shape[0]
    g = hq // hkv
    nb = t // n_seq
    nq = n_seq // tq
    in_specs = [pl.BlockSpec((g, tq, d), lambda b, kv, i: (kv, b * nq + i, 0)),
                pl.BlockSpec((1, n_seq, d), lambda b, kv, i: (kv, b, 0)),
                pl.BlockSpec((1, n_seq, d), lambda b, kv, i: (kv, b, 0))]
    args = [q, k, v]
    if ctx is not None:
        lc = ctx[0].shape[1] // nb
        in_specs += [pl.BlockSpec((1, lc, d), lambda b, kv, i: (kv, b, 0))] * 2
        args += list(ctx)
    return pl.pallas_call(
        functools.partial(_attn_kernel, ctx=ctx is not None),
        grid=(nb, hkv, nq),
        in_specs=in_specs,
        out_specs=pl.BlockSpec((g, tq, d), lambda b, kv, i: (kv, b * nq + i, 0)),
        out_shape=jax.ShapeDtypeStruct((hq, t, d), F32),
        compiler_params=_params("parallel", "parallel", "parallel"),
        name="dense_attention",
    )(*args)


def _na_kernel(q_ref, k_ref, v_ref, kx_ref, vx_ref, toep_ref, o_ref, bias_scr):
    i = pl.program_id(1)
    nk = bias_scr.shape[1]
    rows = k_ref.shape[1] // GRID_W
    first_row = jnp.clip(NA_Q_ROWS * i - NA_WIN_R // 2, 0, rows - NA_K_ROWS)

    @pl.when(pl.program_id(2) == 0)
    def _():
        for a in range(NA_Q_ROWS):
            qr = NA_Q_ROWS * i + a
            r_start = jnp.clip(qr - NA_WIN_R // 2, 0, rows - NA_WIN_R)
            for w in range(NA_K_ROWS):
                kr = first_row + w
                in_window = (kr >= r_start) & (kr < r_start + NA_WIN_R)
                dr = jnp.clip(kr - qr + NA_WIN_R - 1, 0, 2 * NA_WIN_R - 2)
                bias_scr[a * GRID_W:(a + 1) * GRID_W, w * GRID_W:(w + 1) * GRID_W] = (
                    toep_ref[0, dr] + jnp.where(in_window, 0.0, MASKED))

    start = pl.multiple_of(first_row * GRID_W, GRID_W)
    q = q_ref[0]
    k = k_ref[0, pl.ds(start, nk), :]
    v = v_ref[0, pl.ds(start, nk), :]
    s = lax.dot_general(q, k, _NT, preferred_element_type=F32) + bias_scr[...]
    sx = lax.dot_general(q, kx_ref[0], _NT, preferred_element_type=F32)
    mx = jnp.maximum(jnp.max(s, axis=-1, keepdims=True), jnp.max(sx, axis=-1, keepdims=True))
    p = jnp.exp(s - mx)
    px = jnp.exp(sx - mx)
    den = jnp.sum(p, axis=-1, keepdims=True) + jnp.sum(px, axis=-1, keepdims=True)
    o = (jnp.dot(p.astype(BF16), v, preferred_element_type=F32)
         + jnp.dot(px.astype(BF16), vx_ref[0], preferred_element_type=F32))
    o_ref[0] = o / den


def _na_toeplitz(rpb):
    qc = jnp.arange(GRID_W, dtype=jnp.int32)[:, None]
    kc = jnp.arange(GRID_W, dtype=jnp.int32)[None, :]
    c_start = jnp.clip(qc - NA_WIN_C // 2, 0, GRID_W - NA_WIN_C)
    in_window = (kc >= c_start) & (kc < c_start + NA_WIN_C)
    offs = jnp.arange(2 * NA_WIN_C - 1, dtype=jnp.int32)[:, None, None]
    onehot = ((kc - qc + NA_WIN_C - 1)[None] == offs) & in_window[None]
    table = jnp.einsum('hrb,bqk->hrqk', rpb.astype(F32), onehot.astype(F32), precision=HIGHEST)
    return table + jnp.where(in_window, 0.0, MASKED)


def _neighbourhood_attention(q, k, v, kx, vx, toep, n_seq):
    h, t, d = q.shape
    nb = t // n_seq
    tq = NA_Q_ROWS * GRID_W
    nblk = n_seq // tq
    lc = kx.shape[1] // nb
    return pl.pallas_call(
        _na_kernel,
        grid=(h, nblk, nb),
        in_specs=[pl.BlockSpec((1, tq, d), lambda hh, i, b: (hh, b * nblk + i, 0)),
                  pl.BlockSpec((1, n_seq, d), lambda hh, i, b: (hh, b, 0)),
                  pl.BlockSpec((1, n_seq, d), lambda hh, i, b: (hh, b, 0)),
                  pl.BlockSpec((1, lc, d), lambda hh, i, b: (hh, b, 0)),
                  pl.BlockSpec((1, lc, d), lambda hh, i, b: (hh, b, 0)),
                  pl.BlockSpec((1,) + toep.shape[1:], lambda hh, i, b: (hh, 0, 0, 0))],
        out_specs=pl.BlockSpec((1, tq, d), lambda hh, i, b: (hh, b * nblk + i, 0)),
        out_shape=jax.ShapeDtypeStruct((h, t, d), F32),
        scratch_shapes=[pltpu.VMEM((tq, NA_K_ROWS * GRID_W), F32)],
        compiler_params=_params("parallel", "parallel", "arbitrary"),
        name="neighbourhood_attention",
    )(q, k, v, kx, vx, toep)


def _mlstm_kernel(q_ref, k_ref, v_ref, o_ref, g_ref, gbias_ref, gout_ref, c0_ref, m0_ref,
                  h_ref, cn_ref, mn_ref, hs_ref, c_scr, m_scr):
    L = MLSTM_CHUNK
    n = q_ref.shape[1]
    nc = n // L
    hs_ref[...] = jnp.zeros(hs_ref.shape, F32)
    c_scr[...] = c0_ref[0]
    m_scr[...] = m0_ref[0]

    row = lax.broadcasted_iota(jnp.int32, (L, L), 0)
    col = lax.broadcasted_iota(jnp.int32, (L, L), 1)
    masks = (col <= row, col >= row)
    cum_mats = tuple(mk.astype(F32) for mk in masks)
    lane = lax.broadcasted_iota(jnp.int32, (L, LANES), 1)
    eye_l = (lax.broadcasted_iota(jnp.int32, (LANES, LANES), 0)
             == lax.broadcasted_iota(jnp.int32, (LANES, LANES), 1)).astype(F32)
    eye_k = (row == col).astype(BF16)

    def column(xmat, j):
        return jnp.sum(jnp.where(lane == j, xmat, 0.0), axis=1, keepdims=True)

    def body(i, carry):
        for d in range(2):
            c = i if d == 0 else nc - 1 - i
            rows = pl.ds(pl.multiple_of(c * L, L), L)
            pre = g_ref[rows, :] + gbias_ref[...]
            logf = jnp.minimum(pre, 0.0) - jnp.log1p(jnp.exp(-jnp.abs(pre)))
            cum = jnp.dot(cum_mats[d], logf, precision=HIGHEST, preferred_element_type=F32)
            pre_t = lax.dot_general(eye_l, pre, _NT, precision=HIGHEST, preferred_element_type=F32)
            cum_t = lax.dot_general(eye_l, cum, _NT, precision=HIGHEST, preferred_element_type=F32)
            last = L - 1 if d == 0 else 0
            for hh in range(B_HEADS):
                s_idx = d * B_HEADS + hh
                ji = d * 2 * B_HEADS + hh
                jf = ji + B_HEADS
                b_col = column(cum, jf)
                li_col = column(pre, ji)
                b_row = cum_t[jf:jf + 1, :]
                li_row = pre_t[ji:ji + 1, :]
                m_prev = m_scr[s_idx:s_idx + 1, 0:1]
                c_aug = c_scr[s_idx]
                log_d = jnp.where(masks[d], b_col - b_row + li_row, -jnp.inf)
                m_inter = b_col + m_prev
                m_t = jnp.maximum(m_inter, jnp.max(log_d, axis=-1, keepdims=True))
                w_inter = jnp.exp(m_inter - m_t)
                qc = q_ref[hh, rows, :]
                kc = k_ref[hh, rows, :]
                vc = v_ref[hh, rows, :]
                s = lax.dot_general(qc, kc, _NT, preferred_element_type=F32) * jnp.exp(log_d - m_t)
                num = (jnp.dot(s.astype(BF16), vc, preferred_element_type=F32)
                       + w_inter * jnp.dot(qc, c_aug.astype(BF16), preferred_element_type=F32))
                den = column(num, HEAD_DIM)
                hs_ref[hh, rows, :] += num[:, :HEAD_DIM] / jnp.maximum(jnp.abs(den), jnp.exp(-m_t))
                m_new = m_t[last:last + 1, :]
                b_last = b_col[last:last + 1, :]
                w = jnp.exp(b_last - b_col + li_col - m_new)
                decay = jnp.exp(b_last + m_prev - m_new)
                k_t = lax.dot_general(eye_k, kc, _NT, preferred_element_type=F32).astype(BF16)
                wv = (w * vc.astype(F32)).astype(BF16)
                c_scr[s_idx] = decay * c_aug + jnp.dot(k_t, wv, preferred_element_type=F32)
                m_scr[s_idx:s_idx + 1, :] = jnp.broadcast_to(m_new, (1, LANES))
        return carry

    lax.fori_loop(0, nc, body, 0)

    for hh in range(B_HEADS):
        gate = jax.nn.sigmoid(o_ref[:, hh * HEAD_DIM:(hh + 1) * HEAD_DIM])
        h_ref[hh] = _rms(hs_ref[hh], gout_ref[hh]) * gate
    cn_ref[0] = c_scr[...]
    mn_ref[0] = m_scr[...]


def _mlstm(q, k, v, o, g, gate_bias, g_out, c0, m0, n_seq):
    h, t, d = q.shape
    nb = t // n_seq
    ns = 2 * h
    heads = lambda b: (0, b, 0)
    return pl.pallas_call(
        _mlstm_kernel,
        grid=(nb,),
        in_specs=[pl.BlockSpec((h, n_seq, d), heads),
                  pl.BlockSpec((h, n_seq, d), heads),
                  pl.BlockSpec((h, n_seq, LANES), heads),
                  pl.BlockSpec((n_seq, h * d), lambda b: (b, 0)),
                  pl.BlockSpec((n_seq, LANES), lambda b: (b, 0)),
                  pl.BlockSpec((1, LANES), lambda b: (0, 0)),
                  pl.BlockSpec((h, 1, d), lambda b: (0, 0, 0)),
                  pl.BlockSpec((1, ns, d, LANES), lambda b: (b, 0, 0, 0)),
                  pl.BlockSpec((1, ns, LANES), lambda b: (b, 0, 0))],
        out_specs=[pl.BlockSpec((h, n_seq, d), heads),
                   pl.BlockSpec((1, ns, d, LANES), lambda b: (b, 0, 0, 0)),
                   pl.BlockSpec((1, ns, LANES), lambda b: (b, 0, 0))],
        out_shape=[jax.ShapeDtypeStruct((h, t, d), F32),
                   jax.ShapeDtypeStruct((nb, ns, d, LANES), F32),
                   jax.ShapeDtypeStruct((nb, ns, LANES), F32)],
        scratch_shapes=[pltpu.VMEM((h, n_seq, d), F32),
                        pltpu.VMEM((ns, d, LANES), F32),
                        pltpu.VMEM((ns, LANES), F32)],
        compiler_params=_params("parallel"),
        name="mlstm",
    )(q, k, v, o, g, gate_bias, g_out, c0, m0)


def _outproj_kernel(*refs, moe):
    a_ref, b_ref, c_ref, w_ref, x_ref, mod_ref, gpost_ref, gpre_ref = refs[:8]
    pos = 8
    if moe:
        wr_ref = refs[pos]
        pos += 1
    xo_ref, h2_ref = refs[pos:pos + 2]
    pos += 2
    if moe:
        gates_ref = refs[pos]
        pos += 1
    cat_ref = refs[pos]

    off = 0
    for src in (a_ref, b_ref, c_ref):
        for j in range(src.shape[0]):
            cat_ref[:, off:off + HEAD_DIM] = src[j]
            off += HEAD_DIM
    out = jnp.dot(cat_ref[...].astype(BF16), w_ref[...], preferred_element_type=F32)
    m = mod_ref[0]
    xn = x_ref[...] + m[2:3] * _rms(out, gpost_ref[...])
    xo_ref[...] = xn
    h2 = _rms(xn, gpre_ref[...]) * (1.0 + m[4:5]) + m[3:4]
    h2_ref[...] = h2.astype(h2_ref.dtype)
    if moe:
        logits = jnp.dot(h2, wr_ref[...], precision=HIGHEST, preferred_element_type=F32)
        lane = lax.broadcasted_iota(jnp.int32, logits.shape, 1)
        logits = jnp.where(lane < N_EXPERTS, logits, -jnp.inf)
        v1 = jnp.max(logits, axis=-1, keepdims=True)
        i1 = jnp.min(jnp.where(logits == v1, lane, LANES), axis=-1, keepdims=True)
        rest = jnp.where(lane == i1, -jnp.inf, logits)
        v2 = jnp.max(rest, axis=-1, keepdims=True)
        i2 = jnp.min(jnp.where(rest == v2, lane, LANES), axis=-1, keepdims=True)
        e2 = jnp.exp(v2 - v1)
        w1 = 1.0 / (1.0 + e2)
        w2 = e2 / (1.0 + e2)
        gates_ref[...] = jnp.where(lane == i1, w1, 0.0) + jnp.where(lane == i2, w2, 0.0)


def _out_projection(a, b, c, w_out, x, mod, mod_row, g_post, g_pre, w_router):
    t = x.shape[0]
    tm = TOKEN_TILE
    moe = w_router is not None
    row = lambda i: (i, 0)
    fixed2 = lambda i: (0, 0)
    heads = lambda i: (0, i, 0)
    in_specs = [pl.BlockSpec((a.shape[0], tm, HEAD_DIM), heads),
                pl.BlockSpec((b.shape[0], tm, HEAD_DIM), heads),
                pl.BlockSpec((c.shape[0], tm, HEAD_DIM), heads),
                pl.BlockSpec((D_MODEL, D_MODEL), fixed2),
                pl.BlockSpec((tm, D_MODEL), row),
                pl.BlockSpec((1, 6, D_MODEL), lambda i: (mod_row(i), 0, 0)),
                pl.BlockSpec((1, D_MODEL), fixed2),
                pl.BlockSpec((1, D_MODEL), fixed2)]
    args = [a, b, c, w_out, x, mod, g_post, g_pre]
    out_specs = [pl.BlockSpec((tm, D_MODEL), row), pl.BlockSpec((tm, D_MODEL), row)]
    out_shape = [jax.ShapeDtypeStruct((t, D_MODEL), F32),
                 jax.ShapeDtypeStruct((t, D_MODEL), F32 if moe else BF16)]
    if moe:
        in_specs.append(pl.BlockSpec((D_MODEL, LANES), fixed2))
        args.append(w_router)
        out_specs.append(pl.BlockSpec((tm, LANES), row))
        out_shape.append(jax.ShapeDtypeStruct((t, LANES), F32))
    return pl.pallas_call(
        functools.partial(_outproj_kernel, moe=moe),
        grid=(t // tm,),
        in_specs=in_specs,
        out_specs=out_specs,
        out_shape=out_shape,
        scratch_shapes=[pltpu.VMEM((tm, D_MODEL), F32)],
        compiler_params=_params("parallel"),
        name="out_projection",
    )(*args)


def _swiglu_chunk(h, wg, wu, wd):
    a = jnp.dot(h, wg, preferred_element_type=F32)
    b = jnp.dot(h, wu, preferred_element_type=F32)
    return jnp.dot((_silu(a) * b).astype(BF16), wd, preferred_element_type=F32)


def _ffn_kernel(h_ref, wg_ref, wu_ref, wd_ref, x_ref, mod_ref, g_ref, o_ref, acc_ref):
    j = pl.program_id(1)

    @pl.when(j == 0)
    def _():
        acc_ref[...] = jnp.zeros(acc_ref.shape, F32)

    acc_ref[...] += _swiglu_chunk(h_ref[...], wg_ref[...], wu_ref[...], wd_ref[...])

    @pl.when(j == pl.num_programs(1) - 1)
    def _():
        o_ref[...] = x_ref[...] + mod_ref[0][5:6] * _rms(acc_ref[...], g_ref[...])


def _dense_ffn(h2, wg, wu, wd, x, mod, mod_row, g_post):
    t = x.shape[0]
    tm = TOKEN_TILE
    ff = wg.shape[1]
    tf = ff // 2
    return pl.pallas_call(
        _ffn_kernel,
        grid=(t // tm, ff // tf),
        in_specs=[pl.BlockSpec((tm, D_MODEL), lambda i, j: (i, 0)),
                  pl.BlockSpec((D_MODEL, tf), lambda i, j: (0, j)),
                  pl.BlockSpec((D_MODEL, tf), lambda i, j: (0, j)),
                  pl.BlockSpec((tf, D_MODEL), lambda i, j: (j, 0)),
                  pl.BlockSpec((tm, D_MODEL), lambda i, j: (i, 0)),
                  pl.BlockSpec((1, 6, D_MODEL), lambda i, j: (mod_row(i), 0, 0)),
                  pl.BlockSpec((1, D_MODEL), lambda i, j: (0, 0))],
        out_specs=pl.BlockSpec((tm, D_MODEL), lambda i, j: (i, 0)),
        out_shape=jax.ShapeDtypeStruct((t, D_MODEL), F32),
        scratch_shapes=[pltpu.VMEM((tm, D_MODEL), F32)],
        compiler_params=_params("parallel", "arbitrary"),
        name="dense_ffn",
    )(h2, wg, wu, wd, x, mod, g_post)


def _row_copy(src_hbm, src_row, dst, dst_row, sem):
    return pltpu.make_async_copy(src_hbm.at[pl.ds(src_row, 1)], dst.at[pl.ds(dst_row, 1)], sem)


def _gather_rows(src_hbm, idx_ref, base, dst, sem):
    n = dst.shape[0]

    def issue(r, carry):
        _row_copy(src_hbm, idx_ref[base + r], dst, r, sem).start()
        return carry

    def drain(r, carry):
        _row_copy(src_hbm, 0, dst, r, sem).wait()
        return carry

    lax.fori_loop(0, n, issue, 0, unroll=8)
    lax.fori_loop(0, n, drain, 0, unroll=8)


def _moe_kernel(te_ref, nv_ref, tok_ref, h_hbm, w_ref, wg_ref, wu_ref, wd_ref, y_ref,
                xbuf, xb16, acc_ref, sem):
    del te_ref
    i = pl.program_id(0)
    j = pl.program_id(1)
    used = i < nv_ref[0]

    @pl.when(used & (j == 0))
    def _():
        _gather_rows(h_hbm, tok_ref, i * xbuf.shape[0], xbuf, sem)
        xb16[...] = xbuf[...].astype(BF16)
        acc_ref[...] = jnp.zeros(acc_ref.shape, F32)

    @pl.when(used)
    def _():
        acc_ref[...] += _swiglu_chunk(xb16[...], wg_ref[0], wu_ref[0], wd_ref[0])

    last = j == pl.num_programs(1) - 1

    @pl.when(used & last)
    def _():
        y_ref[...] = w_ref[:, 0:1] * acc_ref[...]

    @pl.when(jnp.logical_not(used) & last)
    def _():
        y_ref[...] = jnp.zeros(y_ref.shape, F32)


def _combine_kernel(p0_ref, p1_ref, y_hbm, x_ref, mod_ref, g_ref, o_ref, buf0, buf1, sem0, sem1):
    base = pl.program_id(0) * buf0.shape[0]
    _gather_rows(y_hbm, p0_ref, base, buf0, sem0)
    _gather_rows(y_hbm, p1_ref, base, buf1, sem1)
    f = buf0[...] + buf1[...]
    o_ref[...] = x_ref[...] + mod_ref[0][5:6] * _rms(f, g_ref[...])


def _route(gates, tm, n_tiles):
    t = gates.shape[0]
    g = gates[:, :N_EXPERTS]
    mask = g != 0.0
    mi = mask.astype(jnp.int32)
    rank = jnp.cumsum(mi, axis=0) - mi
    tiles_e = (jnp.sum(mi, axis=0) + tm - 1) // tm
    tile_end = jnp.cumsum(tiles_e)
    n_used = tile_end[-1]
    zero_row = n_tiles * tm
    pos = jnp.where(mask, (tile_end - tiles_e)[None, :] * tm + rank, zero_row + tm)
    tok = jnp.broadcast_to(jnp.arange(t, dtype=jnp.int32)[:, None], pos.shape)
    rows = zero_row + tm
    tok_sorted = jnp.zeros((rows,), jnp.int32).at[pos.ravel()].set(tok.ravel(), mode='drop')
    w_sorted = jnp.zeros((rows,), F32).at[pos.ravel()].set(g.ravel(), mode='drop')
    tile_expert = jnp.minimum(
        jnp.sum(jnp.arange(n_tiles + 1, dtype=jnp.int32)[:, None] >= tile_end[None, :], axis=1),
        N_EXPERTS - 1).astype(jnp.int32)
    order = jnp.cumsum(mi, axis=1)
    p0 = jnp.min(jnp.where(mask & (order == 1), pos, zero_row), axis=1).astype(jnp.int32)
    p1 = jnp.min(jnp.where(mask & (order == 2), pos, zero_row), axis=1).astype(jnp.int32)
    return (tok_sorted, jnp.broadcast_to(w_sorted[:, None], (rows, LANES)), tile_expert,
            n_used.reshape(1).astype(jnp.int32), p0, p1)


def _moe_ffn(h2, gates, wg, wu, wd, x, mod, mod_row, g_post):
    t = x.shape[0]
    tm = TOKEN_TILE
    ne, _, ff = wg.shape
    tf = 512
    nj = ff // tf
    n_tiles = 2 * t // tm + ne
    tok_sorted, w_sorted, tile_expert, n_used, p0, p1 = _route(gates, tm, n_tiles)

    def chunk(i, j, nv):
        return jnp.where(i < nv[0], j, nj - 1)

    y = pl.pallas_call(
        _moe_kernel,
        grid_spec=pltpu.PrefetchScalarGridSpec(
            num_scalar_prefetch=3,
            grid=(n_tiles + 1, nj),
            in_specs=[pl.BlockSpec(memory_space=pl.ANY),
                      pl.BlockSpec((tm, LANES), lambda i, j, te, nv, tok: (i, 0)),
                      pl.BlockSpec((1, D_MODEL, tf), lambda i, j, te, nv, tok: (te[i], 0, chunk(i, j, nv))),
                      pl.BlockSpec((1, D_MODEL, tf), lambda i, j, te, nv, tok: (te[i], 0, chunk(i, j, nv))),
                      pl.BlockSpec((1, tf, D_MODEL), lambda i, j, te, nv, tok: (te[i], chunk(i, j, nv), 0))],
            out_specs=pl.BlockSpec((tm, D_MODEL), lambda i, j, te, nv, tok: (i, 0)),
            scratch_shapes=[pltpu.VMEM((tm, D_MODEL), F32), pltpu.VMEM((tm, D_MODEL), BF16),
                            pltpu.VMEM((tm, D_MODEL), F32), pltpu.SemaphoreType.DMA(())]),
        out_shape=jax.ShapeDtypeStruct(((n_tiles + 1) * tm, D_MODEL), F32),
        compiler_params=_params("arbitrary", "arbitrary"),
        name="moe_experts",
    )(tile_expert, n_used, tok_sorted, h2, w_sorted, wg, wu, wd)

    tc = TOKEN_TILE // 2
    mrow = lambda i: mod_row(i * tc // TOKEN_TILE)
    return pl.pallas_call(
        _combine_kernel,
        grid_spec=pltpu.PrefetchScalarGridSpec(
            num_scalar_prefetch=2,
            grid=(t // tc,),
            in_specs=[pl.BlockSpec(memory_space=pl.ANY),
                      pl.BlockSpec((tc, D_MODEL), lambda i, p0, p1: (i, 0)),
                      pl.BlockSpec((1, 6, D_MODEL), lambda i, p0, p1: (mrow(i), 0, 0)),
                      pl.BlockSpec((1, D_MODEL), lambda i, p0, p1: (0, 0))],
            out_specs=pl.BlockSpec((tc, D_MODEL), lambda i, p0, p1: (i, 0)),
            scratch_shapes=[pltpu.VMEM((tc, D_MODEL), F32), pltpu.VMEM((tc, D_MODEL), F32),
                            pltpu.SemaphoreType.DMA(()), pltpu.SemaphoreType.DMA(())]),
        out_shape=jax.ShapeDtypeStruct((t, D_MODEL), F32),
        compiler_params=_params("arbitrary"),
        name="moe_combine",
    )(p0, p1, y, x, mod, g_post)


def _reorder_w_in(w):
    gate0 = _OFF_BO + B_W
    body = jnp.concatenate([w[:, :gate0], w[:, gate0 + N_GATES:]], axis=1)
    gates = w[:, gate0:gate0 + N_GATES]
    pad = jnp.zeros((w.shape[0], LANES - N_GATES), w.dtype)
    return jnp.concatenate([body, gates, pad], axis=1).astype(BF16)


def _rope_tables(n):
    tok = jnp.arange(n, dtype=jnp.int32)
    pos = jnp.stack([tok // GRID_W, tok % GRID_W], axis=-1).astype(F32)
    inv = ROPE_THETA ** (-jnp.arange(ROPE_QUARTER, dtype=F32) / ROPE_QUARTER)
    ang = pos[:, :, None] * inv
    cos, sin = jnp.cos(ang), jnp.sin(ang)
    cos_t = jnp.concatenate([cos[:, 0], cos[:, 0], cos[:, 1], cos[:, 1]], axis=-1)
    sin_t = jnp.concatenate([-sin[:, 0], sin[:, 0], -sin[:, 1], sin[:, 1]], axis=-1)
    return cos_t, sin_t


def _head_major(cache):
    b, l, h, d = cache.shape
    return jnp.transpose(cache, (2, 0, 1, 3)).reshape(h, b * l, d).astype(BF16)


def kernel(x_prompt, x_sample, cache_gqa_k, cache_gqa_v, cache_na_k, cache_na_v, state_mlstm_C, state_mlstm_n, state_mlstm_m, c, c_ctx, w_mod, b_mod, g_pre_mix, g_post_mix, g_pre_ffn, g_post_ffn, w_in, w_out, g_q, g_k, mlstm_gate_bias, g_mlstm_out, na_rpb, w_ffn_gate, w_ffn_up, w_ffn_down, w_router, w_exp_gate, w_exp_up, w_exp_down):
    batch, seq, _ = x_prompt.shape
    dec_batch, dec_seq, _ = x_sample.shape
    assert seq % TOKEN_TILE == 0 or TOKEN_TILE % seq == 0
    assert dec_seq % (2 * TOKEN_TILE) == 0 and dec_seq % (NA_Q_ROWS * GRID_W) == 0
    tp, ts = batch * seq, dec_batch * dec_seq
    xp = x_prompt.reshape(tp, D_MODEL)
    xs = x_sample.reshape(ts, D_MODEL)

    n_mod = 8
    cvecs = jnp.concatenate([c_ctx[None, :], c, jnp.zeros((n_mod - 1 - dec_batch, D_MODEL), F32)], axis=0)
    mod_all = _modulation(cvecs, w_mod, b_mod)
    prompt_row = lambda i: 0
    tiles_per_sample = dec_seq // TOKEN_TILE
    sample_row = lambda i: 1 + i // tiles_per_sample

    rope_tabs = _rope_tables(dec_seq)
    row2 = lambda v: v.reshape(1, -1)
    ns = 2 * B_HEADS
    zeros_c = jnp.zeros((batch, ns, HEAD_DIM, LANES), F32)
    zeros_m = jnp.zeros((batch, ns, LANES), F32)

    ka_l, va_l, kc_l, vc_l, cst_l, m_l = [], [], [], [], [], []
    for l in range(DEPTH):
        mod = mod_all[l]
        w_in_l = _reorder_w_in(w_in[l])
        w_out_l = w_out[l].astype(BF16)
        gbias = jnp.concatenate([mlstm_gate_bias[l].reshape(1, N_GATES),
                                 jnp.zeros((1, LANES - N_GATES), F32)], axis=1)
        gout = g_mlstm_out[l].reshape(B_HEADS, 1, HEAD_DIM)
        moe = l % 2 == 1
        if moe:
            w_r = jnp.concatenate([w_router[l // 2], jnp.zeros((D_MODEL, LANES - N_EXPERTS), F32)], axis=1)
            ffn_w = tuple(w[l // 2].astype(BF16) for w in (w_exp_gate, w_exp_up, w_exp_down))
        else:
            w_r = None
            ffn_w = tuple(w[l // 2].astype(BF16) for w in (w_ffn_gate, w_ffn_up, w_ffn_down))

        def channel_mixer(res, x_mid, mod_row):
            if moe:
                return _moe_ffn(res[1], res[2], *ffn_w, x_mid, mod, mod_row, row2(g_post_ffn[l]))
            return _dense_ffn(res[1], *ffn_w, x_mid, mod, mod_row, row2(g_post_ffn[l]))

        (qa, ka, va, qb, kb, vb, ob, gb, qc, kc, vc, cka, cva, ckc, cvc) = _in_projection(
            xp, mod, prompt_row, row2(g_pre_mix[l]), w_in_l, row2(g_q[l]), row2(g_k[l]), None, True)
        a_out = _attention(qa, ka, va, seq, seq)
        b_out, cst, mst = _mlstm(qb, kb, vb, ob, gb, gbias, gout, zeros_c, zeros_m, seq)
        c_out = _attention(qc, kc, vc, seq, seq)
        res = _out_projection(a_out, b_out, c_out, w_out_l, xp, mod, prompt_row,
                              row2(g_post_mix[l]), row2(g_pre_ffn[l]), w_r)
        xp = channel_mixer(res, res[0], prompt_row)
        ka_l.append(cka.reshape(batch, seq, A_KV_HEADS, HEAD_DIM))
        va_l.append(cva.reshape(batch, seq, A_KV_HEADS, HEAD_DIM))
        kc_l.append(ckc.reshape(batch, seq, C_HEADS, HEAD_DIM))
        vc_l.append(cvc.reshape(batch, seq, C_HEADS, HEAD_DIM))
        cst_l.append(cst.reshape(batch, 2, B_HEADS, HEAD_DIM, LANES))
        m_l.append(mst.reshape(batch, 2, B_HEADS, LANES)[..., 0])

        (qa, ka, va, qb, kb, vb, ob, gb, qc, kc, vc) = _in_projection(
            xs, mod, sample_row, row2(g_pre_mix[l]), w_in_l, row2(g_q[l]), row2(g_k[l]), rope_tabs, False)
        a_out = _attention(qa, ka, va, dec_seq, 256,
                           ctx=(_head_major(cache_gqa_k[:, l]), _head_major(cache_gqa_v[:, l])))
        c0 = jnp.concatenate([state_mlstm_C[:, l], state_mlstm_n[:, l][..., None],
                              jnp.zeros((dec_batch, 2, B_HEADS, HEAD_DIM, LANES - HEAD_DIM - 1), F32)],
                             axis=-1).reshape(dec_batch, ns, HEAD_DIM, LANES)
        m0 = jnp.broadcast_to(state_mlstm_m[:, l].reshape(dec_batch, ns, 1), (dec_batch, ns, LANES))
        b_out, _, _ = _mlstm(qb, kb, vb, ob, gb, gbias, gout, c0, m0, dec_seq)
        c_out = _neighbourhood_attention(qc, kc, vc, _head_major(cache_na_k[:, l]),
                                         _head_major(cache_na_v[:, l]),
                                         _na_toeplitz(na_rpb[l]), dec_seq)
        res = _out_projection(a_out, b_out, c_out, w_out_l, xs, mod, sample_row,
                              row2(g_post_mix[l]), row2(g_pre_ffn[l]), w_r)
        xs = channel_mixer(res, res[0], sample_row)

    cst = jnp.stack(cst_l, axis=1)
    return (xp.reshape(batch, seq, D_MODEL), xs.reshape(dec_batch, dec_seq, D_MODEL),
            jnp.stack(ka_l, axis=1), jnp.stack(va_l, axis=1),
            jnp.stack(kc_l, axis=1), jnp.stack(vc_l, axis=1),
            cst[..., :HEAD_DIM], cst[..., HEAD_DIM], jnp.stack(m_l, axis=1))
```

```python
import functools

import jax
import jax.numpy as jnp
from jax import lax
from jax.experimental import pallas as pl
from jax.experimental.pallas import tpu as pltpu

D_MODEL = 1024
DEPTH = 2
GRID_W = 64
HEAD_DIM = 64
A_HEADS = 6
A_KV_HEADS = 2
A_GROUP = A_HEADS // A_KV_HEADS
B_HEADS = 4
C_HEADS = 6
A_Q_W = A_HEADS * HEAD_DIM
A_KV_W = A_KV_HEADS * HEAD_DIM
B_W = B_HEADS * HEAD_DIM
C_W = C_HEADS * HEAD_DIM
MLSTM_CHUNK = 64
NA_WIN_R = 8
NA_WIN_C = 16
ROPE_THETA = 10000.0
ROPE_QUARTER = HEAD_DIM // 4
ATTN_SCALE = HEAD_DIM ** -0.5
N_EXPERTS = 8
EPS = 1e-6

LANES = 128
N_GATES = 4 * B_HEADS
MASKED = -1e30

_OFF_AQ = 0
_OFF_AK = _OFF_AQ + A_Q_W
_OFF_AV = _OFF_AK + A_KV_W
_OFF_BQ = _OFF_AV + A_KV_W
_OFF_BK = _OFF_BQ + B_W
_OFF_BV = _OFF_BK + B_W
_OFF_BO = _OFF_BV + B_W
_OFF_CQ = _OFF_BO + B_W
_OFF_CK = _OFF_CQ + C_W
_OFF_CV = _OFF_CK + C_W
_OFF_BG = _OFF_CV + C_W
P_IN_PAD = _OFF_BG + LANES

TOKEN_TILE = 512
NA_Q_ROWS = 8
NA_K_ROWS = 16
MLSTM_STEP_TOKENS = 1024
VMEM_LIMIT = 56 * 1024 * 1024

F32 = jnp.float32
BF16 = jnp.bfloat16
HIGHEST = lax.Precision.HIGHEST
_NT = (((1,), (1,)), ((), ()))


def _params(*sem):
    return pltpu.CompilerParams(dimension_semantics=sem, vmem_limit_bytes=VMEM_LIMIT)


def _rms(x, g):
    return x * lax.rsqrt(jnp.mean(x * x, axis=-1, keepdims=True) + EPS) * g


def _silu(x):
    return x * jax.nn.sigmoid(x)


def _mod_kernel(c_ref, w_ref, b_ref, o_ref):
    s = _silu(c_ref[...])
    o_ref[0] = jnp.dot(s, w_ref[0], precision=HIGHEST, preferred_element_type=F32) + b_ref[0]


def _modulation(cvecs, w_mod, b_mod):
    nb = cvecs.shape[0]
    tn = 1536
    out = pl.pallas_call(
        _mod_kernel,
        grid=(DEPTH, 6 * D_MODEL // tn),
        in_specs=[pl.BlockSpec((nb, D_MODEL), lambda l, j: (0, 0)),
                  pl.BlockSpec((1, D_MODEL, tn), lambda l, j: (l, 0, j)),
                  pl.BlockSpec((1, 1, tn), lambda l, j: (l, 0, j))],
        out_specs=pl.BlockSpec((1, nb, tn), lambda l, j: (l, 0, j)),
        out_shape=jax.ShapeDtypeStruct((DEPTH, nb, 6 * D_MODEL), F32),
        compiler_params=_params("parallel", "parallel"),
        name="adaln_mod",
    )(cvecs, w_mod, b_mod.reshape(DEPTH, 1, 6 * D_MODEL))
    return out.reshape(DEPTH, nb, 6, D_MODEL)


def _inproj_kernel(*refs, rope, caches):
    x_ref, mod_ref, g_ref, w_ref, gq_ref, gk_ref = refs[:6]
    pos = 6
    if rope:
        cos_ref, sin_ref = refs[pos:pos + 2]
        pos += 2
    (qa_ref, ka_ref, va_ref, qb_ref, kb_ref, vb_ref, ob_ref, gb_ref,
     qc_ref, kc_ref, vc_ref) = refs[pos:pos + 11]
    pos += 11
    if caches:
        cka_ref, cva_ref, ckc_ref, cvc_ref = refs[pos:pos + 4]

    m = mod_ref[0]
    h = _rms(x_ref[...], g_ref[...]) * (1.0 + m[1:2]) + m[0:1]
    y = jnp.dot(h.astype(BF16), w_ref[...], preferred_element_type=F32)

    def head(off, j):
        return y[:, off + j * HEAD_DIM: off + (j + 1) * HEAD_DIM]

    def rotary(p):
        if not rope:
            return p
        swapped = jnp.concatenate([p[:, 16:32], p[:, 0:16], p[:, 48:64], p[:, 32:48]], axis=1)
        return p * cos_ref[...] + swapped * sin_ref[...]

    for j in range(A_HEADS):
        qa_ref[j] = (rotary(_rms(head(_OFF_AQ, j), gq_ref[...])) * ATTN_SCALE).astype(BF16)
    for j in range(A_KV_HEADS):
        kn = _rms(head(_OFF_AK, j), gk_ref[...])
        ka_ref[j] = rotary(kn).astype(BF16)
        va_ref[j] = head(_OFF_AV, j).astype(BF16)
        if caches:
            cka_ref[:, j * HEAD_DIM:(j + 1) * HEAD_DIM] = kn
    lane = lax.broadcasted_iota(jnp.int32, (y.shape[0], LANES), 1)
    ones_col = jnp.where(lane == HEAD_DIM, 1.0, 0.0)
    for j in range(B_HEADS):
        qb_ref[j] = (head(_OFF_BQ, j) * ATTN_SCALE).astype(BF16)
        kb_ref[j] = head(_OFF_BK, j).astype(BF16)
        wide = y[:, _OFF_BV + j * HEAD_DIM: _OFF_BV + j * HEAD_DIM + LANES]
        vb_ref[j] = jnp.where(lane < HEAD_DIM, wide, ones_col).astype(BF16)
    ob_ref[...] = y[:, _OFF_BO:_OFF_BO + B_W]
    gb_ref[...] = y[:, _OFF_BG:_OFF_BG + LANES]
    for j in range(C_HEADS):
        qc_ref[j] = (head(_OFF_CQ, j) * ATTN_SCALE).astype(BF16)
        kc_ref[j] = head(_OFF_CK, j).astype(BF16)
        vc_ref[j] = head(_OFF_CV, j).astype(BF16)
    if caches:
        cva_ref[...] = y[:, _OFF_AV:_OFF_AV + A_KV_W]
        ckc_ref[...] = y[:, _OFF_CK:_OFF_CK + C_W]
        cvc_ref[...] = y[:, _OFF_CV:_OFF_CV + C_W]


def _in_projection(x, mod, mod_row, g_pre, w_in, g_q, g_k, rope_tabs, caches):
    t = x.shape[0]
    tm = TOKEN_TILE
    rope = rope_tabs is not None
    row = lambda i: (i, 0)
    heads = lambda i: (0, i, 0)
    fixed2 = lambda i: (0, 0)
    in_specs = [pl.BlockSpec((tm, D_MODEL), row),
                pl.BlockSpec((1, 6, D_MODEL), lambda i: (mod_row(i), 0, 0)),
                pl.BlockSpec((1, D_MODEL), fixed2),
                pl.BlockSpec((D_MODEL, P_IN_PAD), fixed2),
                pl.BlockSpec((1, HEAD_DIM), fixed2),
                pl.BlockSpec((1, HEAD_DIM), fixed2)]
    args = [x, mod, g_pre, w_in, g_q, g_k]
    if rope:
        n_pos = rope_tabs[0].shape[0] // tm
        in_specs += [pl.BlockSpec((tm, HEAD_DIM), lambda i: (i % n_pos, 0))] * 2
        args += list(rope_tabs)

    def hm(nh, width=HEAD_DIM):
        return (jax.ShapeDtypeStruct((nh, t, width), BF16), pl.BlockSpec((nh, tm, width), heads))

    def tokmajor(width):
        return (jax.ShapeDtypeStruct((t, width), F32), pl.BlockSpec((tm, width), row))

    outs = [hm(A_HEADS), hm(A_KV_HEADS), hm(A_KV_HEADS),
            hm(B_HEADS), hm(B_HEADS), hm(B_HEADS, LANES), tokmajor(B_W), tokmajor(LANES),
            hm(C_HEADS), hm(C_HEADS), hm(C_HEADS)]
    if caches:
        outs += [tokmajor(A_KV_W), tokmajor(A_KV_W), tokmajor(C_W), tokmajor(C_W)]
    return pl.pallas_call(
        functools.partial(_inproj_kernel, rope=rope, caches=caches),
        grid=(t // tm,),
        in_specs=in_specs,
        out_specs=[o[1] for o in outs],
        out_shape=[o[0] for o in outs],
        compiler_params=_params("parallel"),
        name="in_projection",
    )(*args)


def _attn_kernel(*refs, ctx):
    if ctx:
        q_ref, k_ref, v_ref, kx_ref, vx_ref, o_ref = refs
    else:
        q_ref, k_ref, v_ref, o_ref = refs
    g, tq, d = q_ref.shape
    q = q_ref[...].reshape(g * tq, d)
    s = lax.dot_general(q, k_ref[0], _NT, preferred_element_type=F32)
    mx = jnp.max(s, axis=-1, keepdims=True)
    if ctx:
        sx = lax.dot_general(q, kx_ref[0], _NT, preferred_element_type=F32)
        mx = jnp.maximum(mx, jnp.max(sx, axis=-1, keepdims=True))
    p = jnp.exp(s - mx)
    den = jnp.sum(p, axis=-1, keepdims=True)
    o = jnp.dot(p.astype(BF16), v_ref[0], preferred_element_type=F32)
    if ctx:
        px = jnp.exp(sx - mx)
        den = den + jnp.sum(px, axis=-1, keepdims=True)
        o = o + jnp.dot(px.astype(BF16), vx_ref[0], preferred_element_type=F32)
    o_ref[...] = (o / den).reshape(g, tq, d)


def _attention(q, k, v, n_seq, tq, ctx=None):
    hq, t, d = q.shape
    hkv = k.shape[0]
    g = hq // hkv
    nb = t // n_seq
    nq = n_seq // tq
    in_specs = [pl.BlockSpec((g, tq, d), lambda b, kv, i: (kv, b * nq + i, 0)),
                pl.BlockSpec((1, n_seq, d), lambda b, kv, i: (kv, b, 0)),
                pl.BlockSpec((1, n_seq, d), lambda b, kv, i: (kv, b, 0))]
    args = [q, k, v]
    if ctx is not None:
        lc = ctx[0].shape[1] // nb
        in_specs += [pl.BlockSpec((1, lc, d), lambda b, kv, i: (kv, b, 0))] * 2
        args += list(ctx)
    return pl.pallas_call(
        functools.partial(_attn_kernel, ctx=ctx is not None),
        grid=(nb, hkv, nq),
        in_specs=in_specs,
        out_specs=pl.BlockSpec((g, tq, d), lambda b, kv, i: (kv, b * nq + i, 0)),
        out_shape=jax.ShapeDtypeStruct((hq, t, d), F32),
        compiler_params=_params("parallel", "parallel", "parallel"),
        name="dense_attention",
    )(*args)


def _na_kernel(q_ref, k_ref, v_ref, kx_ref, vx_ref, toep_ref, o_ref, bias_scr):
    i = pl.program_id(1)
    nk = bias_scr.shape[1]
    rows = k_ref.shape[1] // GRID_W
    first_row = jnp.clip(NA_Q_ROWS * i - NA_WIN_R // 2, 0, rows - NA_K_ROWS)

    @pl.when(pl.program_id(2) == 0)
    def _():
        for a in range(NA_Q_ROWS):
            qr = NA_Q_ROWS * i + a
            r_start = jnp.clip(qr - NA_WIN_R // 2, 0, rows - NA_WIN_R)
            for w in range(NA_K_ROWS):
                kr = first_row + w
                in_window = (kr >= r_start) & (kr < r_start + NA_WIN_R)
                dr = jnp.clip(kr - qr + NA_WIN_R - 1, 0, 2 * NA_WIN_R - 2)
                bias_scr[a * GRID_W:(a + 1) * GRID_W, w * GRID_W:(w + 1) * GRID_W] = (
                    toep_ref[0, dr] + jnp.where(in_window, 0.0, MASKED))

    start = pl.multiple_of(first_row * GRID_W, GRID_W)
    q = q_ref[0]
    k = k_ref[0, pl.ds(start, nk), :]
    v = v_ref[0, pl.ds(start, nk), :]
    s = lax.dot_general(q, k, _NT, preferred_element_type=F32) + bias_scr[...]
    sx = lax.dot_general(q, kx_ref[0], _NT, preferred_element_type=F32)
    mx = jnp.maximum(jnp.max(s, axis=-1, keepdims=True), jnp.max(sx, axis=-1, keepdims=True))
    p = jnp.exp(s - mx)
    px = jnp.exp(sx - mx)
    den = jnp.sum(p, axis=-1, keepdims=True) + jnp.sum(px, axis=-1, keepdims=True)
    o = (jnp.dot(p.astype(BF16), v, preferred_element_type=F32)
         + jnp.dot(px.astype(BF16), vx_ref[0], preferred_element_type=F32))
    o_ref[0] = o / den


def _na_toeplitz(rpb):
    qc = jnp.arange(GRID_W, dtype=jnp.int32)[:, None]
    kc = jnp.arange(GRID_W, dtype=jnp.int32)[None, :]
    c_start = jnp.clip(qc - NA_WIN_C // 2, 0, GRID_W - NA_WIN_C)
    in_window = (kc >= c_start) & (kc < c_start + NA_WIN_C)
    offs = jnp.arange(2 * NA_WIN_C - 1, dtype=jnp.int32)[:, None, None]
    onehot = ((kc - qc + NA_WIN_C - 1)[None] == offs) & in_window[None]
    table = jnp.einsum('hrb,bqk->hrqk', rpb.astype(F32), onehot.astype(F32), precision=HIGHEST)
    return table + jnp.where(in_window, 0.0, MASKED)


def _neighbourhood_attention(q, k, v, kx, vx, toep, n_seq):
    h, t, d = q.shape
    nb = t // n_seq
    tq = NA_Q_ROWS * GRID_W
    nblk = n_seq // tq
    lc = kx.shape[1] // nb
    return pl.pallas_call(
        _na_kernel,
        grid=(h, nblk, nb),
        in_specs=[pl.BlockSpec((1, tq, d), lambda hh, i, b: (hh, b * nblk + i, 0)),
                  pl.BlockSpec((1, n_seq, d), lambda hh, i, b: (hh, b, 0)),
                  pl.BlockSpec((1, n_seq, d), lambda hh, i, b: (hh, b, 0)),
                  pl.BlockSpec((1, lc, d), lambda hh, i, b: (hh, b, 0)),
                  pl.BlockSpec((1, lc, d), lambda hh, i, b: (hh, b, 0)),
                  pl.BlockSpec((1,) + toep.shape[1:], lambda hh, i, b: (hh, 0, 0, 0))],
        out_specs=pl.BlockSpec((1, tq, d), lambda hh, i, b: (hh, b * nblk + i, 0)),
        out_shape=jax.ShapeDtypeStruct((h, t, d), F32),
        scratch_shapes=[pltpu.VMEM((tq, NA_K_ROWS * GRID_W), F32)],
        compiler_params=_params("parallel", "parallel", "arbitrary"),
        name="neighbourhood_attention",
    )(q, k, v, kx, vx, toep)


def _mlstm_kernel(q_ref, k_ref, v_ref, o_ref, g_ref, gbias_ref, gout_ref, c0_ref, m0_ref,
                  h_ref, cn_ref, mn_ref, hs_fwd, hs_bwd):
    L = MLSTM_CHUNK
    nseq, ns = m0_ref.shape[:2]
    n = q_ref.shape[1] // nseq
    nc = n // L
    hs_dir = (hs_fwd, hs_bwd)

    row = lax.broadcasted_iota(jnp.int32, (L, L), 0)
    col = lax.broadcasted_iota(jnp.int32, (L, L), 1)
    masks = (col <= row, col >= row)
    cum_mats = tuple(mk.astype(F32) for mk in masks)
    lane = lax.broadcasted_iota(jnp.int32, (L, LANES), 1)
    eye_l = (lax.broadcasted_iota(jnp.int32, (LANES, LANES), 0)
             == lax.broadcasted_iota(jnp.int32, (LANES, LANES), 1)).astype(F32)
    eye_k = (row == col).astype(BF16)

    def column(xmat, j):
        return jnp.sum(jnp.where(lane == j, xmat, 0.0), axis=1, keepdims=True)

    def body(i, carry):
        cs, ms = list(carry[0]), list(carry[1])
        groups = [(sq, d) for sq in range(nseq) for d in range(2)]
        streams = [(sq, d, hh) for sq, d in groups for hh in range(B_HEADS)]
        rows_of, pre_of, cum_of, pre_t_of, cum_t_of = {}, {}, {}, {}, {}
        for sq, d in groups:
            c = i if d == 0 else nc - 1 - i
            rows_of[sq, d] = pl.ds(pl.multiple_of(sq * n + c * L, L), L)
            pre = g_ref[rows_of[sq, d], :] + gbias_ref[...]
            logf = jnp.minimum(pre, 0.0) - jnp.log1p(jnp.exp(-jnp.abs(pre)))
            pre_of[sq, d] = pre
            cum_of[sq, d] = jnp.dot(cum_mats[d], logf, precision=HIGHEST, preferred_element_type=F32)
        for key in groups:
            pre_t_of[key] = lax.dot_general(eye_l, pre_of[key], _NT, precision=HIGHEST,
                                            preferred_element_type=F32)
            cum_t_of[key] = lax.dot_general(eye_l, cum_of[key], _NT, precision=HIGHEST,
                                            preferred_element_type=F32)

        idx_of = {st: st[0] * ns + st[1] * B_HEADS + st[2] for st in streams}
        qkv, qk_of, qc_of, kt_of = {}, {}, {}, {}
        for st in streams:
            sq, d, hh = st
            rows = rows_of[sq, d]
            qc, kc, vc = q_ref[hh, rows, :], k_ref[hh, rows, :], v_ref[hh, rows, :]
            qkv[st] = (qc, kc, vc)
            qk_of[st] = lax.dot_general(qc, kc, _NT, preferred_element_type=F32)
            kt_of[st] = lax.dot_general(eye_k, kc, _NT, preferred_element_type=F32).astype(BF16)
        for st in streams:
            qc_of[st] = jnp.dot(qkv[st][0], cs[idx_of[st]].astype(BF16), preferred_element_type=F32)

        gate_of = {}
        for st in streams:
            sq, d, hh = st
            ji = d * 2 * B_HEADS + hh
            jf = ji + B_HEADS
            b_col = column(cum_of[sq, d], jf)
            li_col = column(pre_of[sq, d], ji)
            b_row = cum_t_of[sq, d][jf:jf + 1, :]
            li_row = pre_t_of[sq, d][ji:ji + 1, :]
            m_prev = ms[idx_of[st]][:, 0:1]
            log_d = jnp.where(masks[d], b_col - b_row + li_row, -jnp.inf)
            m_inter = b_col + m_prev
            m_t = jnp.maximum(m_inter, jnp.max(log_d, axis=-1, keepdims=True))
            gate_of[st] = (b_col, li_col, m_prev, m_t, jnp.exp(m_inter - m_t), jnp.exp(log_d - m_t))

        num_of = {}
        for st in streams:
            w_inter, decay_mat = gate_of[st][4], gate_of[st][5]
            s = qk_of[st] * decay_mat
            num_of[st] = (jnp.dot(s.astype(BF16), qkv[st][2], preferred_element_type=F32)
                          + w_inter * qc_of[st])

        for st in streams:
            sq, d, hh = st
            b_col, li_col, m_prev, m_t = gate_of[st][:4]
            num = num_of[st]
            den = column(num, HEAD_DIM)
            hs_dir[d][hh, rows_of[sq, d], :] = (num[:, :HEAD_DIM]
                                               / jnp.maximum(jnp.abs(den), jnp.exp(-m_t)))
            last = L - 1 if d == 0 else 0
            m_new = m_t[last:last + 1, :]
            b_last = b_col[last:last + 1, :]
            w = jnp.exp(b_last - b_col + li_col - m_new)
            decay = jnp.exp(b_last + m_prev - m_new)
            wv = (w * qkv[st][2].astype(F32)).astype(BF16)
            cs[idx_of[st]] = decay * cs[idx_of[st]] + jnp.dot(kt_of[st], wv, preferred_element_type=F32)
            ms[idx_of[st]] = jnp.broadcast_to(m_new, (1, LANES))
        return tuple(cs), tuple(ms)

    streams = [(sq, s) for sq in range(nseq) for s in range(ns)]
    cs, ms = lax.fori_loop(0, nc, body, (tuple(c0_ref[sq, s] for sq, s in streams),
                                         tuple(m0_ref[sq, s:s + 1, :] for sq, s in streams)))

    for hh in range(B_HEADS):
        gate = jax.nn.sigmoid(o_ref[:, hh * HEAD_DIM:(hh + 1) * HEAD_DIM])
        h_ref[hh] = _rms(hs_fwd[hh] + hs_bwd[hh], gout_ref[hh]) * gate
    for idx, (sq, s) in enumerate(streams):
        cn_ref[sq, s] = cs[idx]
        mn_ref[sq, s:s + 1, :] = ms[idx]


def _mlstm(q, k, v, o, g, gate_bias, g_out, c0, m0, n_seq):
    h, t, d = q.shape
    nsq = max(1, MLSTM_STEP_TOKENS // n_seq)
    nb = t // n_seq
    assert nb % nsq == 0
    ns = 2 * h
    heads = lambda b: (0, b, 0)
    return pl.pallas_call(
        _mlstm_kernel,
        grid=(nb // nsq,),
        in_specs=[pl.BlockSpec((h, nsq * n_seq, d), heads),
                  pl.BlockSpec((h, nsq * n_seq, d), heads),
                  pl.BlockSpec((h, nsq * n_seq, LANES), heads),
                  pl.BlockSpec((nsq * n_seq, h * d), lambda b: (b, 0)),
                  pl.BlockSpec((nsq * n_seq, LANES), lambda b: (b, 0)),
                  pl.BlockSpec((1, LANES), lambda b: (0, 0)),
                  pl.BlockSpec((h, 1, d), lambda b: (0, 0, 0)),
                  pl.BlockSpec((nsq, ns, d, LANES), lambda b: (b, 0, 0, 0)),
                  pl.BlockSpec((nsq, ns, LANES), lambda b: (b, 0, 0))],
        out_specs=[pl.BlockSpec((h, nsq * n_seq, d), heads),
                   pl.BlockSpec((nsq, ns, d, LANES), lambda b: (b, 0, 0, 0)),
                   pl.BlockSpec((nsq, ns, LANES), lambda b: (b, 0, 0))],
        out_shape=[jax.ShapeDtypeStruct((h, t, d), F32),
                   jax.ShapeDtypeStruct((nb, ns, d, LANES), F32),
                   jax.ShapeDtypeStruct((nb, ns, LANES), F32)],
        scratch_shapes=[pltpu.VMEM((h, nsq * n_seq, d), F32), pltpu.VMEM((h, nsq * n_seq, d), F32)],
        compiler_params=_params("parallel"),
        name="mlstm",
    )(q, k, v, o, g, gate_bias, g_out, c0, m0)


def _outproj_kernel(*refs, moe):
    a_ref, b_ref, c_ref, w_ref, x_ref, mod_ref, gpost_ref, gpre_ref = refs[:8]
    pos = 8
    if moe:
        wr_ref = refs[pos]
        pos += 1
    xo_ref, h2_ref = refs[pos:pos + 2]
    pos += 2
    if moe:
        gates_ref = refs[pos]
        pos += 1
    cat_ref = refs[pos]

    off = 0
    for src in (a_ref, b_ref, c_ref):
        for j in range(src.shape[0]):
            cat_ref[:, off:off + HEAD_DIM] = src[j]
            off += HEAD_DIM
    out = jnp.dot(cat_ref[...].astype(BF16), w_ref[...], preferred_element_type=F32)
    m = mod_ref[0]
    xn = x_ref[...] + m[2:3] * _rms(out, gpost_ref[...])
    xo_ref[...] = xn
    h2 = _rms(xn, gpre_ref[...]) * (1.0 + m[4:5]) + m[3:4]
    h2_ref[...] = h2.astype(h2_ref.dtype)
    if moe:
        logits = jnp.dot(h2, wr_ref[...], precision=HIGHEST, preferred_element_type=F32)
        lane = lax.broadcasted_iota(jnp.int32, logits.shape, 1)
        logits = jnp.where(lane < N_EXPERTS, logits, -jnp.inf)
        v1 = jnp.max(logits, axis=-1, keepdims=True)
        i1 = jnp.min(jnp.where(logits == v1, lane, LANES), axis=-1, keepdims=True)
        rest = jnp.where(lane == i1, -jnp.inf, logits)
        v2 = jnp.max(rest, axis=-1, keepdims=True)
        i2 = jnp.min(jnp.where(rest == v2, lane, LANES), axis=-1, keepdims=True)
        e2 = jnp.exp(v2 - v1)
        w1 = 1.0 / (1.0 + e2)
        w2 = e2 / (1.0 + e2)
        gates_ref[...] = jnp.where(lane == i1, w1, 0.0) + jnp.where(lane == i2, w2, 0.0)


def _out_projection(a, b, c, w_out, x, mod, mod_row, g_post, g_pre, w_router):
    t = x.shape[0]
    tm = TOKEN_TILE
    moe = w_router is not None
    row = lambda i: (i, 0)
    fixed2 = lambda i: (0, 0)
    heads = lambda i: (0, i, 0)
    in_specs = [pl.BlockSpec((a.shape[0], tm, HEAD_DIM), heads),
                pl.BlockSpec((b.shape[0], tm, HEAD_DIM), heads),
                pl.BlockSpec((c.shape[0], tm, HEAD_DIM), heads),
                pl.BlockSpec((D_MODEL, D_MODEL), fixed2),
                pl.BlockSpec((tm, D_MODEL), row),
                pl.BlockSpec((1, 6, D_MODEL), lambda i: (mod_row(i), 0, 0)),
                pl.BlockSpec((1, D_MODEL), fixed2),
                pl.BlockSpec((1, D_MODEL), fixed2)]
    args = [a, b, c, w_out, x, mod, g_post, g_pre]
    out_specs = [pl.BlockSpec((tm, D_MODEL), row), pl.BlockSpec((tm, D_MODEL), row)]
    out_shape = [jax.ShapeDtypeStruct((t, D_MODEL), F32),
                 jax.ShapeDtypeStruct((t, D_MODEL), F32 if moe else BF16)]
    if moe:
        in_specs.append(pl.BlockSpec((D_MODEL, LANES), fixed2))
        args.append(w_router)
        out_specs.append(pl.BlockSpec((tm, LANES), row))
        out_shape.append(jax.ShapeDtypeStruct((t, LANES), F32))
    return pl.pallas_call(
        functools.partial(_outproj_kernel, moe=moe),
        grid=(t // tm,),
        in_specs=in_specs,
        out_specs=out_specs,
        out_shape=out_shape,
        scratch_shapes=[pltpu.VMEM((tm, D_MODEL), F32)],
        compiler_params=_params("parallel"),
        name="out_projection",
    )(*args)


def _swiglu_chunk(h, wg, wu, wd):
    a = jnp.dot(h, wg, preferred_element_type=F32)
    b = jnp.dot(h, wu, preferred_element_type=F32)
    return jnp.dot((_silu(a) * b).astype(BF16), wd, preferred_element_type=F32)


def _ffn_kernel(h_ref, wg_ref, wu_ref, wd_ref, x_ref, mod_ref, g_ref, o_ref, acc_ref):
    j = pl.program_id(1)

    @pl.when(j == 0)
    def _():
        acc_ref[...] = jnp.zeros(acc_ref.shape, F32)

    acc_ref[...] += _swiglu_chunk(h_ref[...], wg_ref[...], wu_ref[...], wd_ref[...])

    @pl.when(j == pl.num_programs(1) - 1)
    def _():
        o_ref[...] = x_ref[...] + mod_ref[0][5:6] * _rms(acc_ref[...], g_ref[...])


def _dense_ffn(h2, wg, wu, wd, x, mod, mod_row, g_post):
    t = x.shape[0]
    tm = TOKEN_TILE
    ff = wg.shape[1]
    tf = ff // 2
    return pl.pallas_call(
        _ffn_kernel,
        grid=(t // tm, ff // tf),
        in_specs=[pl.BlockSpec((tm, D_MODEL), lambda i, j: (i, 0)),
                  pl.BlockSpec((D_MODEL, tf), lambda i, j: (0, j)),
                  pl.BlockSpec((D_MODEL, tf), lambda i, j: (0, j)),
                  pl.BlockSpec((tf, D_MODEL), lambda i, j: (j, 0)),
                  pl.BlockSpec((tm, D_MODEL), lambda i, j: (i, 0)),
                  pl.BlockSpec((1, 6, D_MODEL), lambda i, j: (mod_row(i), 0, 0)),
                  pl.BlockSpec((1, D_MODEL), lambda i, j: (0, 0))],
        out_specs=pl.BlockSpec((tm, D_MODEL), lambda i, j: (i, 0)),
        out_shape=jax.ShapeDtypeStruct((t, D_MODEL), F32),
        scratch_shapes=[pltpu.VMEM((tm, D_MODEL), F32)],
        compiler_params=_params("parallel", "arbitrary"),
        name="dense_ffn",
    )(h2, wg, wu, wd, x, mod, g_post)


def _row_copy(src_hbm, src_row, dst, dst_row, sem):
    return pltpu.make_async_copy(src_hbm.at[pl.ds(src_row, 1)], dst.at[pl.ds(dst_row, 1)], sem)


def _gather_rows(src_hbm, idx_ref, base, dst, sem):
    n = dst.shape[0]

    def issue(r, carry):
        _row_copy(src_hbm, idx_ref[base + r], dst, r, sem).start()
        return carry

    def drain(r, carry):
        _row_copy(src_hbm, 0, dst, r, sem).wait()
        return carry

    lax.fori_loop(0, n, issue, 0, unroll=8)
    lax.fori_loop(0, n, drain, 0, unroll=8)


def _moe_kernel(te_ref, nv_ref, tok_ref, h_hbm, w_ref, wg_ref, wu_ref, wd_ref, y_ref,
                xbuf, xb16, acc_ref, sem):
    del te_ref
    i = pl.program_id(0)
    j = pl.program_id(1)
    used = i < nv_ref[0]

    @pl.when(used & (j == 0))
    def _():
        _gather_rows(h_hbm, tok_ref, i * xbuf.shape[0], xbuf, sem)
        xb16[...] = xbuf[...].astype(BF16)
        acc_ref[...] = jnp.zeros(acc_ref.shape, F32)

    @pl.when(used)
    def _():
        acc_ref[...] += _swiglu_chunk(xb16[...], wg_ref[0], wu_ref[0], wd_ref[0])

    last = j == pl.num_programs(1) - 1

    @pl.when(used & last)
    def _():
        y_ref[...] = w_ref[:, 0:1] * acc_ref[...]

    @pl.when(jnp.logical_not(used) & last)
    def _():
        y_ref[...] = jnp.zeros(y_ref.shape, F32)


def _combine_kernel(p0_ref, p1_ref, y_hbm, x_ref, mod_ref, g_ref, o_ref, buf0, buf1, sem0, sem1):
    base = pl.program_id(0) * buf0.shape[0]
    _gather_rows(y_hbm, p0_ref, base, buf0, sem0)
    _gather_rows(y_hbm, p1_ref, base, buf1, sem1)
    f = buf0[...] + buf1[...]
    o_ref[...] = x_ref[...] + mod_ref[0][5:6] * _rms(f, g_ref[...])


def _route(gates, tm, n_tiles):
    t = gates.shape[0]
    g = gates[:, :N_EXPERTS]
    mask = g != 0.0
    mi = mask.astype(jnp.int32)
    rank = jnp.cumsum(mi, axis=0) - mi
    tiles_e = (jnp.sum(mi, axis=0) + tm - 1) // tm
    tile_end = jnp.cumsum(tiles_e)
    n_used = tile_end[-1]
    zero_row = n_tiles * tm
    pos = (tile_end - tiles_e)[None, :] * tm + rank
    rows = zero_row + tm
    tile_expert = jnp.minimum(
        jnp.sum(jnp.arange(n_tiles + 1, dtype=jnp.int32)[:, None] >= tile_end[None, :], axis=1),
        N_EXPERTS - 1).astype(jnp.int32)
    order = jnp.cumsum(mi, axis=1)
    first, second = mask & (order == 1), mask & (order == 2)
    p0 = jnp.min(jnp.where(first, pos, zero_row), axis=1).astype(jnp.int32)
    p1 = jnp.min(jnp.where(second, pos, zero_row), axis=1).astype(jnp.int32)
    w0 = jnp.sum(jnp.where(first, g, 0.0), axis=1)
    w1 = jnp.sum(jnp.where(second, g, 0.0), axis=1)
    tok = jnp.arange(t, dtype=jnp.int32)
    dest = jnp.concatenate([p0, p1])
    tok_sorted = jnp.zeros((rows,), jnp.int32).at[dest].set(jnp.concatenate([tok, tok]))
    w_sorted = jnp.zeros((rows,), F32).at[dest].set(jnp.concatenate([w0, w1]))
    return (tok_sorted, jnp.broadcast_to(w_sorted[:, None], (rows, LANES)), tile_expert,
            n_used.reshape(1).astype(jnp.int32), p0, p1)


def _moe_ffn(h2, gates, wg, wu, wd, x, mod, mod_row, g_post):
    t = x.shape[0]
    tm = TOKEN_TILE
    ne, _, ff = wg.shape
    tf = 512
    nj = ff // tf
    n_tiles = 2 * t // tm + ne
    tok_sorted, w_sorted, tile_expert, n_used, p0, p1 = _route(gates, tm, n_tiles)

    def chunk(i, j, nv):
        return jnp.where(i < nv[0], j, nj - 1)

    y = pl.pallas_call(
        _moe_kernel,
        grid_spec=pltpu.PrefetchScalarGridSpec(
            num_scalar_prefetch=3,
            grid=(n_tiles + 1, nj),
            in_specs=[pl.BlockSpec(memory_space=pl.ANY),
                      pl.BlockSpec((tm, LANES), lambda i, j, te, nv, tok: (i, 0)),
                      pl.BlockSpec((1, D_MODEL, tf), lambda i, j, te, nv, tok: (te[i], 0, chunk(i, j, nv))),
                      pl.BlockSpec((1, D_MODEL, tf), lambda i, j, te, nv, tok: (te[i], 0, chunk(i, j, nv))),
                      pl.BlockSpec((1, tf, D_MODEL), lambda i, j, te, nv, tok: (te[i], chunk(i, j, nv), 0))],
            out_specs=pl.BlockSpec((tm, D_MODEL), lambda i, j, te, nv, tok: (i, 0)),
            scratch_shapes=[pltpu.VMEM((tm, D_MODEL), F32), pltpu.VMEM((tm, D_MODEL), BF16),
                            pltpu.VMEM((tm, D_MODEL), F32), pltpu.SemaphoreType.DMA(())]),
        out_shape=jax.ShapeDtypeStruct(((n_tiles + 1) * tm, D_MODEL), F32),
        compiler_params=_params("arbitrary", "arbitrary"),
        name="moe_experts",
    )(tile_expert, n_used, tok_sorted, h2, w_sorted, wg, wu, wd)

    tc = TOKEN_TILE // 2
    mrow = lambda i: mod_row(i * tc // TOKEN_TILE)
    return pl.pallas_call(
        _combine_kernel,
        grid_spec=pltpu.PrefetchScalarGridSpec(
            num_scalar_prefetch=2,
            grid=(t // tc,),
            in_specs=[pl.BlockSpec(memory_space=pl.ANY),
                      pl.BlockSpec((tc, D_MODEL), lambda i, p0, p1: (i, 0)),
                      pl.BlockSpec((1, 6, D_MODEL), lambda i, p0, p1: (mrow(i), 0, 0)),
                      pl.BlockSpec((1, D_MODEL), lambda i, p0, p1: (0, 0))],
            out_specs=pl.BlockSpec((tc, D_MODEL), lambda i, p0, p1: (i, 0)),
            scratch_shapes=[pltpu.VMEM((tc, D_MODEL), F32), pltpu.VMEM((tc, D_MODEL), F32),
                            pltpu.SemaphoreType.DMA(()), pltpu.SemaphoreType.DMA(())]),
        out_shape=jax.ShapeDtypeStruct((t, D_MODEL), F32),
        compiler_params=_params("arbitrary"),
        name="moe_combine",
    )(p0, p1, y, x, mod, g_post)


def _reorder_w_in(w):
    gate0 = _OFF_BO + B_W
    body = jnp.concatenate([w[:, :gate0], w[:, gate0 + N_GATES:]], axis=1)
    gates = w[:, gate0:gate0 + N_GATES]
    pad = jnp.zeros((w.shape[0], LANES - N_GATES), w.dtype)
    return jnp.concatenate([body, gates, pad], axis=1).astype(BF16)


def _rope_tables(n):
    tok = jnp.arange(n, dtype=jnp.int32)
    pos = jnp.stack([tok // GRID_W, tok % GRID_W], axis=-1).astype(F32)
    inv = ROPE_THETA ** (-jnp.arange(ROPE_QUARTER, dtype=F32) / ROPE_QUARTER)
    ang = pos[:, :, None] * inv
    cos, sin = jnp.cos(ang), jnp.sin(ang)
    cos_t = jnp.concatenate([cos[:, 0], cos[:, 0], cos[:, 1], cos[:, 1]], axis=-1)
    sin_t = jnp.concatenate([-sin[:, 0], sin[:, 0], -sin[:, 1], sin[:, 1]], axis=-1)
    return cos_t, sin_t


def _head_major(cache):
    b, l, h, d = cache.shape
    return jnp.transpose(cache, (2, 0, 1, 3)).reshape(h, b * l, d).astype(BF16)


def kernel(x_prompt, x_sample, cache_gqa_k, cache_gqa_v, cache_na_k, cache_na_v, state_mlstm_C, state_mlstm_n, state_mlstm_m, c, c_ctx, w_mod, b_mod, g_pre_mix, g_post_mix, g_pre_ffn, g_post_ffn, w_in, w_out, g_q, g_k, mlstm_gate_bias, g_mlstm_out, na_rpb, w_ffn_gate, w_ffn_up, w_ffn_down, w_router, w_exp_gate, w_exp_up, w_exp_down):
    batch, seq, _ = x_prompt.shape
    dec_batch, dec_seq, _ = x_sample.shape
    assert seq % TOKEN_TILE == 0 or TOKEN_TILE % seq == 0
    assert dec_seq % (2 * TOKEN_TILE) == 0 and dec_seq % (NA_Q_ROWS * GRID_W) == 0
    tp, ts = batch * seq, dec_batch * dec_seq
    xp = x_prompt.reshape(tp, D_MODEL)
    xs = x_sample.reshape(ts, D_MODEL)

    n_mod = 8
    cvecs = jnp.concatenate([c_ctx[None, :], c, jnp.zeros((n_mod - 1 - dec_batch, D_MODEL), F32)], axis=0)
    mod_all = _modulation(cvecs, w_mod, b_mod)
    prompt_row = lambda i: 0
    tiles_per_sample = dec_seq // TOKEN_TILE
    sample_row = lambda i: 1 + i // tiles_per_sample

    rope_tabs = _rope_tables(dec_seq)
    row2 = lambda v: v.reshape(1, -1)
    ns = 2 * B_HEADS
    zeros_c = jnp.zeros((batch, ns, HEAD_DIM, LANES), F32)
    zeros_m = jnp.zeros((batch, ns, LANES), F32)

    ka_l, va_l, kc_l, vc_l, cst_l, m_l = [], [], [], [], [], []
    for l in range(DEPTH):
        mod = mod_all[l]
        w_in_l = _reorder_w_in(w_in[l])
        w_out_l = w_out[l].astype(BF16)
        gbias = jnp.concatenate([mlstm_gate_bias[l].reshape(1, N_GATES),
                                 jnp.zeros((1, LANES - N_GATES), F32)], axis=1)
        gout = g_mlstm_out[l].reshape(B_HEADS, 1, HEAD_DIM)
        moe = l % 2 == 1
        if moe:
            w_r = jnp.concatenate([w_router[l // 2], jnp.zeros((D_MODEL, LANES - N_EXPERTS), F32)], axis=1)
            ffn_w = tuple(w[l // 2].astype(BF16) for w in (w_exp_gate, w_exp_up, w_exp_down))
        else:
            w_r = None
            ffn_w = tuple(w[l // 2].astype(BF16) for w in (w_ffn_gate, w_ffn_up, w_ffn_down))

        def channel_mixer(res, x_mid, mod_row):
            if moe:
                return _moe_ffn(res[1], res[2], *ffn_w, x_mid, mod, mod_row, row2(g_post_ffn[l]))
            return _dense_ffn(res[1], *ffn_w, x_mid, mod, mod_row, row2(g_post_ffn[l]))

        (qa, ka, va, qb, kb, vb, ob, gb, qc, kc, vc, cka, cva, ckc, cvc) = _in_projection(
            xp, mod, prompt_row, row2(g_pre_mix[l]), w_in_l, row2(g_q[l]), row2(g_k[l]), None, True)
        a_out = _attention(qa, ka, va, seq, seq)
        b_out, cst, mst = _mlstm(qb, kb, vb, ob, gb, gbias, gout, zeros_c, zeros_m, seq)
        c_out = _attention(qc, kc, vc, seq, seq)
        res = _out_projection(a_out, b_out, c_out, w_out_l, xp, mod, prompt_row,
                              row2(g_post_mix[l]), row2(g_pre_ffn[l]), w_r)
        xp = channel_mixer(res, res[0], prompt_row)
        ka_l.append(cka.reshape(batch, seq, A_KV_HEADS, HEAD_DIM))
        va_l.append(cva.reshape(batch, seq, A_KV_HEADS, HEAD_DIM))
        kc_l.append(ckc.reshape(batch, seq, C_HEADS, HEAD_DIM))
        vc_l.append(cvc.reshape(batch, seq, C_HEADS, HEAD_DIM))
        cst_l.append(cst.reshape(batch, 2, B_HEADS, HEAD_DIM, LANES))
        m_l.append(mst.reshape(batch, 2, B_HEADS, LANES)[..., 0])

        (qa, ka, va, qb, kb, vb, ob, gb, qc, kc, vc) = _in_projection(
            xs, mod, sample_row, row2(g_pre_mix[l]), w_in_l, row2(g_q[l]), row2(g_k[l]), rope_tabs, False)
        a_out = _attention(qa, ka, va, dec_seq, 256,
                           ctx=(_head_major(cache_gqa_k[:, l]), _head_major(cache_gqa_v[:, l])))
        c0 = jnp.concatenate([state_mlstm_C[:, l], state_mlstm_n[:, l][..., None],
                              jnp.zeros((dec_batch, 2, B_HEADS, HEAD_DIM, LANES - HEAD_DIM - 1), F32)],
                             axis=-1).reshape(dec_batch, ns, HEAD_DIM, LANES)
        m0 = jnp.broadcast_to(state_mlstm_m[:, l].reshape(dec_batch, ns, 1), (dec_batch, ns, LANES))
        b_out, _, _ = _mlstm(qb, kb, vb, ob, gb, gbias, gout, c0, m0, dec_seq)
        c_out = _neighbourhood_attention(qc, kc, vc, _head_major(cache_na_k[:, l]),
                                         _head_major(cache_na_v[:, l]),
                                         _na_toeplitz(na_rpb[l]), dec_seq)
        res = _out_projection(a_out, b_out, c_out, w_out_l, xs, mod, sample_row,
                              row2(g_post_mix[l]), row2(g_pre_ffn[l]), w_r)
        xs = channel_mixer(res, res[0], sample_row)

    cst = jnp.stack(cst_l, axis=1)
    return (xp.reshape(batch, seq, D_MODEL), xs.reshape(dec_batch, dec_seq, D_MODEL),
            jnp.stack(ka_l, axis=1), jnp.stack(va_l, axis=1),
            jnp.stack(kc_l, axis=1), jnp.stack(vc_l, axis=1),
            cst[..., :HEAD_DIM], cst[..., HEAD_DIM], jnp.stack(m_l, axis=1))
```

```python
import functools

import jax
import jax.numpy as jnp
from jax import lax
from jax.experimental import pallas as pl
from jax.experimental.pallas import tpu as pltpu

D_MODEL = 1024
DEPTH = 2
GRID_W = 64
HEAD_DIM = 64
A_HEADS = 6
A_KV_HEADS = 2
A_GROUP = A_HEADS // A_KV_HEADS
B_HEADS = 4
C_HEADS = 6
A_Q_W = A_HEADS * HEAD_DIM
A_KV_W = A_KV_HEADS * HEAD_DIM
B_W = B_HEADS * HEAD_DIM
C_W = C_HEADS * HEAD_DIM
MLSTM_CHUNK = 64
NA_WIN_R = 8
NA_WIN_C = 16
ROPE_THETA = 10000.0
ROPE_QUARTER = HEAD_DIM // 4
ATTN_SCALE = HEAD_DIM ** -0.5
N_EXPERTS = 8
EPS = 1e-6

LANES = 128
N_GATES = 4 * B_HEADS
MASKED = -1e30

_OFF_AQ = 0
_OFF_AK = _OFF_AQ + A_Q_W
_OFF_AV = _OFF_AK + A_KV_W
_OFF_BQ = _OFF_AV + A_KV_W
_OFF_BK = _OFF_BQ + B_W
_OFF_BV = _OFF_BK + B_W
_OFF_BO = _OFF_BV + B_W
_OFF_CQ = _OFF_BO + B_W
_OFF_CK = _OFF_CQ + C_W
_OFF_CV = _OFF_CK + C_W
_OFF_BG = _OFF_CV + C_W
P_IN_PAD = _OFF_BG + LANES

TOKEN_TILE = 512
NA_Q_ROWS = 8
NA_K_ROWS = 16
ATTN_ROW_PARTS = 2
MLSTM_STEP_TOKENS = 1024
VMEM_LIMIT = 56 * 1024 * 1024

F32 = jnp.float32
BF16 = jnp.bfloat16
HIGHEST = lax.Precision.HIGHEST
_NT = (((1,), (1,)), ((), ()))


def _params(*sem):
    return pltpu.CompilerParams(dimension_semantics=sem, vmem_limit_bytes=VMEM_LIMIT)


def _rms(x, g):
    return x * lax.rsqrt(jnp.mean(x * x, axis=-1, keepdims=True) + EPS) * g


def _silu(x):
    return x * jax.nn.sigmoid(x)


def _mod_kernel(c_ref, w_ref, b_ref, o_ref):
    s = _silu(c_ref[...])
    o_ref[0] = jnp.dot(s, w_ref[0], precision=HIGHEST, preferred_element_type=F32) + b_ref[0]


def _modulation(cvecs, w_mod, b_mod):
    nb = cvecs.shape[0]
    tn = 1536
    out = pl.pallas_call(
        _mod_kernel,
        grid=(DEPTH, 6 * D_MODEL // tn),
        in_specs=[pl.BlockSpec((nb, D_MODEL), lambda l, j: (0, 0)),
                  pl.BlockSpec((1, D_MODEL, tn), lambda l, j: (l, 0, j)),
                  pl.BlockSpec((1, 1, tn), lambda l, j: (l, 0, j))],
        out_specs=pl.BlockSpec((1, nb, tn), lambda l, j: (l, 0, j)),
        out_shape=jax.ShapeDtypeStruct((DEPTH, nb, 6 * D_MODEL), F32),
        compiler_params=_params("parallel", "parallel"),
        name="adaln_mod",
    )(cvecs, w_mod, b_mod.reshape(DEPTH, 1, 6 * D_MODEL))
    return out.reshape(DEPTH, nb, 6, D_MODEL)


def _inproj_kernel(*refs, rope, caches):
    x_ref, mod_ref, g_ref, w_ref, gq_ref, gk_ref = refs[:6]
    pos = 6
    if rope:
        cos_ref, sin_ref = refs[pos:pos + 2]
        pos += 2
    (qa_ref, ka_ref, va_ref, qb_ref, kb_ref, vb_ref, ob_ref, gb_ref,
     qc_ref, kc_ref, vc_ref) = refs[pos:pos + 11]
    pos += 11
    if caches:
        cka_ref, cva_ref, ckc_ref, cvc_ref = refs[pos:pos + 4]

    m = mod_ref[0]
    h = _rms(x_ref[...], g_ref[...]) * (1.0 + m[1:2]) + m[0:1]
    y = jnp.dot(h.astype(BF16), w_ref[...], preferred_element_type=F32)

    def head(off, j):
        return y[:, off + j * HEAD_DIM: off + (j + 1) * HEAD_DIM]

    def rotary(p):
        if not rope:
            return p
        swapped = jnp.concatenate([p[:, 16:32], p[:, 0:16], p[:, 48:64], p[:, 32:48]], axis=1)
        return p * cos_ref[...] + swapped * sin_ref[...]

    for j in range(A_HEADS):
        qa_ref[j] = (rotary(_rms(head(_OFF_AQ, j), gq_ref[...])) * ATTN_SCALE).astype(BF16)
    for j in range(A_KV_HEADS):
        kn = _rms(head(_OFF_AK, j), gk_ref[...])
        ka_ref[j] = rotary(kn).astype(BF16)
        va_ref[j] = head(_OFF_AV, j).astype(BF16)
        if caches:
            cka_ref[:, j * HEAD_DIM:(j + 1) * HEAD_DIM] = kn
    lane = lax.broadcasted_iota(jnp.int32, (y.shape[0], LANES), 1)
    ones_col = jnp.where(lane == HEAD_DIM, 1.0, 0.0)
    for j in range(B_HEADS):
        qb_ref[j] = (head(_OFF_BQ, j) * ATTN_SCALE).astype(BF16)
        kb_ref[j] = head(_OFF_BK, j).astype(BF16)
        wide = y[:, _OFF_BV + j * HEAD_DIM: _OFF_BV + j * HEAD_DIM + LANES]
        vb_ref[j] = jnp.where(lane < HEAD_DIM, wide, ones_col).astype(BF16)
    ob_ref[...] = y[:, _OFF_BO:_OFF_BO + B_W]
    gb_ref[...] = y[:, _OFF_BG:_OFF_BG + LANES]
    for j in range(C_HEADS):
        qc_ref[j] = (head(_OFF_CQ, j) * ATTN_SCALE).astype(BF16)
        kc_ref[j] = head(_OFF_CK, j).astype(BF16)
        vc_ref[j] = head(_OFF_CV, j).astype(BF16)
    if caches:
        cva_ref[...] = y[:, _OFF_AV:_OFF_AV + A_KV_W]
        ckc_ref[...] = y[:, _OFF_CK:_OFF_CK + C_W]
        cvc_ref[...] = y[:, _OFF_CV:_OFF_CV + C_W]


def _in_projection(x, mod, mod_row, g_pre, w_in, g_q, g_k, rope_tabs, caches):
    t = x.shape[0]
    tm = TOKEN_TILE
    rope = rope_tabs is not None
    row = lambda i: (i, 0)
    heads = lambda i: (0, i, 0)
    fixed2 = lambda i: (0, 0)
    in_specs = [pl.BlockSpec((tm, D_MODEL), row),
                pl.BlockSpec((1, 6, D_MODEL), lambda i: (mod_row(i), 0, 0)),
                pl.BlockSpec((1, D_MODEL), fixed2),
                pl.BlockSpec((D_MODEL, P_IN_PAD), fixed2),
                pl.BlockSpec((1, HEAD_DIM), fixed2),
                pl.BlockSpec((1, HEAD_DIM), fixed2)]
    args = [x, mod, g_pre, w_in, g_q, g_k]
    if rope:
        n_pos = rope_tabs[0].shape[0] // tm
        in_specs += [pl.BlockSpec((tm, HEAD_DIM), lambda i: (i % n_pos, 0))] * 2
        args += list(rope_tabs)

    def hm(nh, width=HEAD_DIM):
        return (jax.ShapeDtypeStruct((nh, t, width), BF16), pl.BlockSpec((nh, tm, width), heads))

    def tokmajor(width):
        return (jax.ShapeDtypeStruct((t, width), F32), pl.BlockSpec((tm, width), row))

    outs = [hm(A_HEADS), hm(A_KV_HEADS), hm(A_KV_HEADS),
            hm(B_HEADS), hm(B_HEADS), hm(B_HEADS, LANES), tokmajor(B_W), tokmajor(LANES),
            hm(C_HEADS), hm(C_HEADS), hm(C_HEADS)]
    if caches:
        outs += [tokmajor(A_KV_W), tokmajor(A_KV_W), tokmajor(C_W), tokmajor(C_W)]
    return pl.pallas_call(
        functools.partial(_inproj_kernel, rope=rope, caches=caches),
        grid=(t // tm,),
        in_specs=in_specs,
        out_specs=[o[1] for o in outs],
        out_shape=[o[0] for o in outs],
        compiler_params=_params("parallel"),
        name="in_projection",
    )(*args)


def _softmax_attend(q, keys, values, bias_ref):
    step = q.shape[0] // ATTN_ROW_PARTS
    scores = []
    for part in range(ATTN_ROW_PARTS):
        qp = q[part * step:(part + 1) * step]
        sc = [lax.dot_general(qp, kb, _NT, preferred_element_type=F32) for kb in keys]
        if bias_ref is not None:
            sc[0] = sc[0] + bias_ref[part * step:(part + 1) * step, :]
        scores.append(sc)
    probs = []
    for sc in scores:
        mx = functools.reduce(jnp.maximum, [jnp.max(s, axis=-1, keepdims=True) for s in sc])
        ps = [jnp.exp(s - mx) for s in sc]
        den = functools.reduce(lambda a, b: a + b, [jnp.sum(p, axis=-1, keepdims=True) for p in ps])
        probs.append((ps, den))
    outs = []
    for ps, den in probs:
        o = functools.reduce(lambda a, b: a + b,
                             [jnp.dot(p.astype(BF16), vb, preferred_element_type=F32)
                              for p, vb in zip(ps, values)])
        outs.append(o / den)
    return jnp.concatenate(outs, axis=0)


def _attn_kernel(*refs, ctx):
    if ctx:
        q_ref, k_ref, v_ref, kx_ref, vx_ref, o_ref = refs
    else:
        q_ref, k_ref, v_ref, o_ref = refs
    g, tq, d = q_ref.shape
    q = q_ref[...].reshape(g * tq, d)
    keys, values = [k_ref[0]], [v_ref[0]]
    if ctx:
        keys.append(kx_ref[0])
        values.append(vx_ref[0])
    o_ref[...] = _softmax_attend(q, keys, values, None).reshape(g, tq, d)


def _attention(q, k, v, n_seq, tq, ctx=None):
    hq, t, d = q.shape
    hkv = k.shape[0]
    g = hq // hkv
    nb = t // n_seq
    nq = n_seq // tq
    in_specs = [pl.BlockSpec((g, tq, d), lambda b, kv, i: (kv, b * nq + i, 0)),
                pl.BlockSpec((1, n_seq, d), lambda b, kv, i: (kv, b, 0)),
                pl.BlockSpec((1, n_seq, d), lambda b, kv, i: (kv, b, 0))]
    args = [q, k, v]
    if ctx is not None:
        lc = ctx[0].shape[1] // nb
        in_specs += [pl.BlockSpec((1, lc, d), lambda b, kv, i: (kv, b, 0))] * 2
        args += list(ctx)
    return pl.pallas_call(
        functools.partial(_attn_kernel, ctx=ctx is not None),
        grid=(nb, hkv, nq),
        in_specs=in_specs,
        out_specs=pl.BlockSpec((g, tq, d), lambda b, kv, i: (kv, b * nq + i, 0)),
        out_shape=jax.ShapeDtypeStruct((hq, t, d), F32),
        compiler_params=_params("parallel", "parallel", "parallel"),
        name="dense_attention",
    )(*args)


def _na_kernel(q_ref, k_ref, v_ref, kx_ref, vx_ref, toep_ref, o_ref, bias_scr):
    i = pl.program_id(1)
    nk = bias_scr.shape[1]
    rows = k_ref.shape[1] // GRID_W
    first_row = jnp.clip(NA_Q_ROWS * i - NA_WIN_R // 2, 0, rows - NA_K_ROWS)

    @pl.when(pl.program_id(2) == 0)
    def _():
        for a in range(NA_Q_ROWS):
            qr = NA_Q_ROWS * i + a
            r_start = jnp.clip(qr - NA_WIN_R // 2, 0, rows - NA_WIN_R)
            for w in range(NA_K_ROWS):
                kr = first_row + w
                in_window = (kr >= r_start) & (kr < r_start + NA_WIN_R)
                dr = jnp.clip(kr - qr + NA_WIN_R - 1, 0, 2 * NA_WIN_R - 2)
                bias_scr[a * GRID_W:(a + 1) * GRID_W, w * GRID_W:(w + 1) * GRID_W] = (
                    toep_ref[0, dr] + jnp.where(in_window, 0.0, MASKED))

    start = pl.multiple_of(first_row * GRID_W, GRID_W)
    q = q_ref[0]
    k = k_ref[0, pl.ds(start, nk), :]
    v = v_ref[0, pl.ds(start, nk), :]
    o_ref[0] = _softmax_attend(q, [k, kx_ref[0]], [v, vx_ref[0]], bias_scr)


def _na_toeplitz(rpb):
    qc = jnp.arange(GRID_W, dtype=jnp.int32)[:, None]
    kc = jnp.arange(GRID_W, dtype=jnp.int32)[None, :]
    c_start = jnp.clip(qc - NA_WIN_C // 2, 0, GRID_W - NA_WIN_C)
    in_window = (kc >= c_start) & (kc < c_start + NA_WIN_C)
    offs = jnp.arange(2 * NA_WIN_C - 1, dtype=jnp.int32)[:, None, None]
    onehot = ((kc - qc + NA_WIN_C - 1)[None] == offs) & in_window[None]
    table = jnp.einsum('hrb,bqk->hrqk', rpb.astype(F32), onehot.astype(F32), precision=HIGHEST)
    return table + jnp.where(in_window, 0.0, MASKED)


def _neighbourhood_attention(q, k, v, kx, vx, toep, n_seq):
    h, t, d = q.shape
    nb = t // n_seq
    tq = NA_Q_ROWS * GRID_W
    nblk = n_seq // tq
    lc = kx.shape[1] // nb
    return pl.pallas_call(
        _na_kernel,
        grid=(h, nblk, nb),
        in_specs=[pl.BlockSpec((1, tq, d), lambda hh, i, b: (hh, b * nblk + i, 0)),
                  pl.BlockSpec((1, n_seq, d), lambda hh, i, b: (hh, b, 0)),
                  pl.BlockSpec((1, n_seq, d), lambda hh, i, b: (hh, b, 0)),
                  pl.BlockSpec((1, lc, d), lambda hh, i, b: (hh, b, 0)),
                  pl.BlockSpec((1, lc, d), lambda hh, i, b: (hh, b, 0)),
                  pl.BlockSpec((1,) + toep.shape[1:], lambda hh, i, b: (hh, 0, 0, 0))],
        out_specs=pl.BlockSpec((1, tq, d), lambda hh, i, b: (hh, b * nblk + i, 0)),
        out_shape=jax.ShapeDtypeStruct((h, t, d), F32),
        scratch_shapes=[pltpu.VMEM((tq, NA_K_ROWS * GRID_W), F32)],
        compiler_params=_params("parallel", "parallel", "arbitrary"),
        name="neighbourhood_attention",
    )(q, k, v, kx, vx, toep)


def _mlstm_kernel(q_ref, k_ref, v_ref, o_ref, g_ref, gbias_ref, gout_ref, c0_ref, m0_ref,
                  h_ref, cn_ref, mn_ref, hs_fwd, hs_bwd):
    L = MLSTM_CHUNK
    nseq, ns = m0_ref.shape[:2]
    n = q_ref.shape[1] // nseq
    nc = n // L
    hs_dir = (hs_fwd, hs_bwd)

    row = lax.broadcasted_iota(jnp.int32, (L, L), 0)
    col = lax.broadcasted_iota(jnp.int32, (L, L), 1)
    masks = (col <= row, col >= row)
    cum_mats = tuple(mk.astype(F32) for mk in masks)
    lane = lax.broadcasted_iota(jnp.int32, (L, LANES), 1)
    eye_l = (lax.broadcasted_iota(jnp.int32, (LANES, LANES), 0)
             == lax.broadcasted_iota(jnp.int32, (LANES, LANES), 1)).astype(F32)
    eye_k = (row == col).astype(BF16)

    def column(xmat, j):
        return jnp.sum(jnp.where(lane == j, xmat, 0.0), axis=1, keepdims=True)

    def body(i, carry):
        cs, ms = list(carry[0]), list(carry[1])
        groups = [(sq, d) for sq in range(nseq) for d in range(2)]
        streams = [(sq, d, hh) for sq, d in groups for hh in range(B_HEADS)]
        rows_of, pre_of, cum_of, pre_t_of, cum_t_of = {}, {}, {}, {}, {}
        for sq, d in groups:
            c = i if d == 0 else nc - 1 - i
            rows_of[sq, d] = pl.ds(pl.multiple_of(sq * n + c * L, L), L)
            pre = g_ref[rows_of[sq, d], :] + gbias_ref[...]
            logf = jnp.minimum(pre, 0.0) - jnp.log1p(jnp.exp(-jnp.abs(pre)))
            pre_of[sq, d] = pre
            cum_of[sq, d] = jnp.dot(cum_mats[d], logf, precision=HIGHEST, preferred_element_type=F32)
        for key in groups:
            pre_t_of[key] = lax.dot_general(eye_l, pre_of[key], _NT, precision=HIGHEST,
                                            preferred_element_type=F32)
            cum_t_of[key] = lax.dot_general(eye_l, cum_of[key], _NT, precision=HIGHEST,
                                            preferred_element_type=F32)

        idx_of = {st: st[0] * ns + st[1] * B_HEADS + st[2] for st in streams}
        qkv, qk_of, qc_of, kt_of = {}, {}, {}, {}
        for st in streams:
            sq, d, hh = st
            rows = rows_of[sq, d]
            qc, kc, vc = q_ref[hh, rows, :], k_ref[hh, rows, :], v_ref[hh, rows, :]
            qkv[st] = (qc, kc, vc)
            qk_of[st] = lax.dot_general(qc, kc, _NT, preferred_element_type=F32)
            kt_of[st] = lax.dot_general(eye_k, kc, _NT, preferred_element_type=F32).astype(BF16)
        for st in streams:
            qc_of[st] = jnp.dot(qkv[st][0], cs[idx_of[st]].astype(BF16), preferred_element_type=F32)

        gate_of = {}
        for st in streams:
            sq, d, hh = st
            ji = d * 2 * B_HEADS + hh
            jf = ji + B_HEADS
            b_col = column(cum_of[sq, d], jf)
            li_col = column(pre_of[sq, d], ji)
            b_row = cum_t_of[sq, d][jf:jf + 1, :]
            li_row = pre_t_of[sq, d][ji:ji + 1, :]
            m_prev = ms[idx_of[st]][:, 0:1]
            log_d = jnp.where(masks[d], b_col - b_row + li_row, -jnp.inf)
            m_inter = b_col + m_prev
            m_t = jnp.maximum(m_inter, jnp.max(log_d, axis=-1, keepdims=True))
            gate_of[st] = (b_col, li_col, m_prev, m_t, jnp.exp(m_inter - m_t), jnp.exp(log_d - m_t))

        num_of = {}
        for st in streams:
            w_inter, decay_mat = gate_of[st][4], gate_of[st][5]
            s = qk_of[st] * decay_mat
            num_of[st] = (jnp.dot(s.astype(BF16), qkv[st][2], preferred_element_type=F32)
                          + w_inter * qc_of[st])

        for st in streams:
            sq, d, hh = st
            b_col, li_col, m_prev, m_t = gate_of[st][:4]
            num = num_of[st]
            den = column(num, HEAD_DIM)
            hs_dir[d][hh, rows_of[sq, d], :] = (num[:, :HEAD_DIM]
                                               / jnp.maximum(jnp.abs(den), jnp.exp(-m_t)))
            last = L - 1 if d == 0 else 0
            m_new = m_t[last:last + 1, :]
            b_last = b_col[last:last + 1, :]
            w = jnp.exp(b_last - b_col + li_col - m_new)
            decay = jnp.exp(b_last + m_prev - m_new)
            wv = (w * qkv[st][2].astype(F32)).astype(BF16)
            cs[idx_of[st]] = decay * cs[idx_of[st]] + jnp.dot(kt_of[st], wv, preferred_element_type=F32)
            ms[idx_of[st]] = jnp.broadcast_to(m_new, (1, LANES))
        return tuple(cs), tuple(ms)

    streams = [(sq, s) for sq in range(nseq) for s in range(ns)]
    cs, ms = lax.fori_loop(0, nc, body, (tuple(c0_ref[sq, s] for sq, s in streams),
                                         tuple(m0_ref[sq, s:s + 1, :] for sq, s in streams)))

    for hh in range(B_HEADS):
        gate = jax.nn.sigmoid(o_ref[:, hh * HEAD_DIM:(hh + 1) * HEAD_DIM])
        h_ref[hh] = _rms(hs_fwd[hh] + hs_bwd[hh], gout_ref[hh]) * gate
    for idx, (sq, s) in enumerate(streams):
        cn_ref[sq, s] = cs[idx]
        mn_ref[sq, s:s + 1, :] = ms[idx]


def _mlstm(q, k, v, o, g, gate_bias, g_out, c0, m0, n_seq):
    h, t, d = q.shape
    nsq = max(1, MLSTM_STEP_TOKENS // n_seq)
    nb = t // n_seq
    assert nb % nsq == 0
    ns = 2 * h
    heads = lambda b: (0, b, 0)
    return pl.pallas_call(
        _mlstm_kernel,
        grid=(nb // nsq,),
        in_specs=[pl.BlockSpec((h, nsq * n_seq, d), heads),
                  pl.BlockSpec((h, nsq * n_seq, d), heads),
                  pl.BlockSpec((h, nsq * n_seq, LANES), heads),
                  pl.BlockSpec((nsq * n_seq, h * d), lambda b: (b, 0)),
                  pl.BlockSpec((nsq * n_seq, LANES), lambda b: (b, 0)),
                  pl.BlockSpec((1, LANES), lambda b: (0, 0)),
                  pl.BlockSpec((h, 1, d), lambda b: (0, 0, 0)),
                  pl.BlockSpec((nsq, ns, d, LANES), lambda b: (b, 0, 0, 0)),
                  pl.BlockSpec((nsq, ns, LANES), lambda b: (b, 0, 0))],
        out_specs=[pl.BlockSpec((h, nsq * n_seq, d), heads),
                   pl.BlockSpec((nsq, ns, d, LANES), lambda b: (b, 0, 0, 0)),
                   pl.BlockSpec((nsq, ns, LANES), lambda b: (b, 0, 0))],
        out_shape=[jax.ShapeDtypeStruct((h, t, d), F32),
                   jax.ShapeDtypeStruct((nb, ns, d, LANES), F32),
                   jax.ShapeDtypeStruct((nb, ns, LANES), F32)],
        scratch_shapes=[pltpu.VMEM((h, nsq * n_seq, d), F32), pltpu.VMEM((h, nsq * n_seq, d), F32)],
        compiler_params=_params("parallel"),
        name="mlstm",
    )(q, k, v, o, g, gate_bias, g_out, c0, m0)


def _outproj_kernel(*refs, moe):
    a_ref, b_ref, c_ref, w_ref, x_ref, mod_ref, gpost_ref, gpre_ref = refs[:8]
    pos = 8
    if moe:
        wr_ref = refs[pos]
        pos += 1
    xo_ref, h2_ref = refs[pos:pos + 2]
    pos += 2
    if moe:
        gates_ref = refs[pos]
        pos += 1
    cat_ref = refs[pos]

    off = 0
    for src in (a_ref, b_ref, c_ref):
        for j in range(src.shape[0]):
            cat_ref[:, off:off + HEAD_DIM] = src[j]
            off += HEAD_DIM
    out = jnp.dot(cat_ref[...].astype(BF16), w_ref[...], preferred_element_type=F32)
    m = mod_ref[0]
    xn = x_ref[...] + m[2:3] * _rms(out, gpost_ref[...])
    xo_ref[...] = xn
    h2 = _rms(xn, gpre_ref[...]) * (1.0 + m[4:5]) + m[3:4]
    h2_ref[...] = h2.astype(h2_ref.dtype)
    if moe:
        logits = jnp.dot(h2, wr_ref[...], precision=HIGHEST, preferred_element_type=F32)
        lane = lax.broadcasted_iota(jnp.int32, logits.shape, 1)
        logits = jnp.where(lane < N_EXPERTS, logits, -jnp.inf)
        v1 = jnp.max(logits, axis=-1, keepdims=True)
        i1 = jnp.min(jnp.where(logits == v1, lane, LANES), axis=-1, keepdims=True)
        rest = jnp.where(lane == i1, -jnp.inf, logits)
        v2 = jnp.max(rest, axis=-1, keepdims=True)
        i2 = jnp.min(jnp.where(rest == v2, lane, LANES), axis=-1, keepdims=True)
        e2 = jnp.exp(v2 - v1)
        w1 = 1.0 / (1.0 + e2)
        w2 = e2 / (1.0 + e2)
        gates_ref[...] = jnp.where(lane == i1, w1, 0.0) + jnp.where(lane == i2, w2, 0.0)


def _out_projection(a, b, c, w_out, x, mod, mod_row, g_post, g_pre, w_router):
    t = x.shape[0]
    tm = TOKEN_TILE
    moe = w_router is not None
    row = lambda i: (i, 0)
    fixed2 = lambda i: (0, 0)
    heads = lambda i: (0, i, 0)
    in_specs = [pl.BlockSpec((a.shape[0], tm, HEAD_DIM), heads),
                pl.BlockSpec((b.shape[0], tm, HEAD_DIM), heads),
                pl.BlockSpec((c.shape[0], tm, HEAD_DIM), heads),
                pl.BlockSpec((D_MODEL, D_MODEL), fixed2),
                pl.BlockSpec((tm, D_MODEL), row),
                pl.BlockSpec((1, 6, D_MODEL), lambda i: (mod_row(i), 0, 0)),
                pl.BlockSpec((1, D_MODEL), fixed2),
                pl.BlockSpec((1, D_MODEL), fixed2)]
    args = [a, b, c, w_out, x, mod, g_post, g_pre]
    out_specs = [pl.BlockSpec((tm, D_MODEL), row), pl.BlockSpec((tm, D_MODEL), row)]
    out_shape = [jax.ShapeDtypeStruct((t, D_MODEL), F32),
                 jax.ShapeDtypeStruct((t, D_MODEL), F32 if moe else BF16)]
    if moe:
        in_specs.append(pl.BlockSpec((D_MODEL, LANES), fixed2))
        args.append(w_router)
        out_specs.append(pl.BlockSpec((tm, LANES), row))
        out_shape.append(jax.ShapeDtypeStruct((t, LANES), F32))
    return pl.pallas_call(
        functools.partial(_outproj_kernel, moe=moe),
        grid=(t // tm,),
        in_specs=in_specs,
        out_specs=out_specs,
        out_shape=out_shape,
        scratch_shapes=[pltpu.VMEM((tm, D_MODEL), F32)],
        compiler_params=_params("parallel"),
        name="out_projection",
    )(*args)


def _swiglu_chunk(h, wg, wu, wd):
    a = jnp.dot(h, wg, preferred_element_type=F32)
    b = jnp.dot(h, wu, preferred_element_type=F32)
    return jnp.dot((_silu(a) * b).astype(BF16), wd, preferred_element_type=F32)


def _ffn_kernel(h_ref, wg_ref, wu_ref, wd_ref, x_ref, mod_ref, g_ref, o_ref, acc_ref):
    j = pl.program_id(1)

    @pl.when(j == 0)
    def _():
        acc_ref[...] = jnp.zeros(acc_ref.shape, F32)

    acc_ref[...] += _swiglu_chunk(h_ref[...], wg_ref[...], wu_ref[...], wd_ref[...])

    @pl.when(j == pl.num_programs(1) - 1)
    def _():
        o_ref[...] = x_ref[...] + mod_ref[0][5:6] * _rms(acc_ref[...], g_ref[...])


def _dense_ffn(h2, wg, wu, wd, x, mod, mod_row, g_post):
    t = x.shape[0]
    tm = TOKEN_TILE
    ff = wg.shape[1]
    tf = ff // 2
    return pl.pallas_call(
        _ffn_kernel,
        grid=(t // tm, ff // tf),
        in_specs=[pl.BlockSpec((tm, D_MODEL), lambda i, j: (i, 0)),
                  pl.BlockSpec((D_MODEL, tf), lambda i, j: (0, j)),
                  pl.BlockSpec((D_MODEL, tf), lambda i, j: (0, j)),
                  pl.BlockSpec((tf, D_MODEL), lambda i, j: (j, 0)),
                  pl.BlockSpec((tm, D_MODEL), lambda i, j: (i, 0)),
                  pl.BlockSpec((1, 6, D_MODEL), lambda i, j: (mod_row(i), 0, 0)),
                  pl.BlockSpec((1, D_MODEL), lambda i, j: (0, 0))],
        out_specs=pl.BlockSpec((tm, D_MODEL), lambda i, j: (i, 0)),
        out_shape=jax.ShapeDtypeStruct((t, D_MODEL), F32),
        scratch_shapes=[pltpu.VMEM((tm, D_MODEL), F32)],
        compiler_params=_params("parallel", "arbitrary"),
        name="dense_ffn",
    )(h2, wg, wu, wd, x, mod, g_post)


def _row_copy(src_hbm, src_row, dst, dst_row, sem):
    return pltpu.make_async_copy(src_hbm.at[pl.ds(src_row, 1)], dst.at[pl.ds(dst_row, 1)], sem)


def _gather_rows(src_hbm, idx_ref, base, dst, sem):
    n = dst.shape[0]

    def issue(r, carry):
        _row_copy(src_hbm, idx_ref[base + r], dst, r, sem).start()
        return carry

    def drain(r, carry):
        _row_copy(src_hbm, 0, dst, r, sem).wait()
        return carry

    lax.fori_loop(0, n, issue, 0, unroll=8)
    lax.fori_loop(0, n, drain, 0, unroll=8)


def _moe_kernel(te_ref, nv_ref, tok_ref, h_hbm, w_ref, wg_ref, wu_ref, wd_ref, y_ref,
                xbuf, xb16, acc_ref, sem):
    del te_ref
    i = pl.program_id(0)
    j = pl.program_id(1)
    used = i < nv_ref[0]

    @pl.when(used & (j == 0))
    def _():
        _gather_rows(h_hbm, tok_ref, i * xbuf.shape[0], xbuf, sem)
        xb16[...] = xbuf[...].astype(BF16)
        acc_ref[...] = jnp.zeros(acc_ref.shape, F32)

    @pl.when(used)
    def _():
        acc_ref[...] += _swiglu_chunk(xb16[...], wg_ref[0], wu_ref[0], wd_ref[0])

    last = j == pl.num_programs(1) - 1

    @pl.when(used & last)
    def _():
        y_ref[...] = w_ref[:, 0:1] * acc_ref[...]

    @pl.when(jnp.logical_not(used) & last)
    def _():
        y_ref[...] = jnp.zeros(y_ref.shape, F32)


def _combine_kernel(p0_ref, p1_ref, y_hbm, x_ref, mod_ref, g_ref, o_ref, buf0, buf1, sem0, sem1):
    base = pl.program_id(0) * buf0.shape[0]
    _gather_rows(y_hbm, p0_ref, base, buf0, sem0)
    _gather_rows(y_hbm, p1_ref, base, buf1, sem1)
    f = buf0[...] + buf1[...]
    o_ref[...] = x_ref[...] + mod_ref[0][5:6] * _rms(f, g_ref[...])


def _route(gates, tm, n_tiles):
    t = gates.shape[0]
    g = gates[:, :N_EXPERTS]
    mask = g != 0.0
    mi = mask.astype(jnp.int32)
    rank = jnp.cumsum(mi, axis=0) - mi
    tiles_e = (jnp.sum(mi, axis=0) + tm - 1) // tm
    tile_end = jnp.cumsum(tiles_e)
    n_used = tile_end[-1]
    zero_row = n_tiles * tm
    pos = (tile_end - tiles_e)[None, :] * tm + rank
    rows = zero_row + tm
    tile_expert = jnp.minimum(
        jnp.sum(jnp.arange(n_tiles + 1, dtype=jnp.int32)[:, None] >= tile_end[None, :], axis=1),
        N_EXPERTS - 1).astype(jnp.int32)
    order = jnp.cumsum(mi, axis=1)
    first, second = mask & (order == 1), mask & (order == 2)
    p0 = jnp.min(jnp.where(first, pos, zero_row), axis=1).astype(jnp.int32)
    p1 = jnp.min(jnp.where(second, pos, zero_row), axis=1).astype(jnp.int32)
    w0 = jnp.sum(jnp.where(first, g, 0.0), axis=1)
    w1 = jnp.sum(jnp.where(second, g, 0.0), axis=1)
    tok = jnp.arange(t, dtype=jnp.int32)
    dest = jnp.concatenate([p0, p1])
    tok_sorted = jnp.zeros((rows,), jnp.int32).at[dest].set(jnp.concatenate([tok, tok]))
    w_sorted = jnp.zeros((rows,), F32).at[dest].set(jnp.concatenate([w0, w1]))
    return (tok_sorted, jnp.broadcast_to(w_sorted[:, None], (rows, LANES)), tile_expert,
            n_used.reshape(1).astype(jnp.int32), p0, p1)


def _moe_ffn(h2, gates, wg, wu, wd, x, mod, mod_row, g_post):
    t = x.shape[0]
    tm = TOKEN_TILE
    ne, _, ff = wg.shape
    tf = 512
    nj = ff // tf
    n_tiles = 2 * t // tm + ne
    tok_sorted, w_sorted, tile_expert, n_used, p0, p1 = _route(gates, tm, n_tiles)

    def chunk(i, j, nv):
        return jnp.where(i < nv[0], j, nj - 1)

    y = pl.pallas_call(
        _moe_kernel,
        grid_spec=pltpu.PrefetchScalarGridSpec(
            num_scalar_prefetch=3,
            grid=(n_tiles + 1, nj),
            in_specs=[pl.BlockSpec(memory_space=pl.ANY),
                      pl.BlockSpec((tm, LANES), lambda i, j, te, nv, tok: (i, 0)),
                      pl.BlockSpec((1, D_MODEL, tf), lambda i, j, te, nv, tok: (te[i], 0, chunk(i, j, nv))),
                      pl.BlockSpec((1, D_MODEL, tf), lambda i, j, te, nv, tok: (te[i], 0, chunk(i, j, nv))),
                      pl.BlockSpec((1, tf, D_MODEL), lambda i, j, te, nv, tok: (te[i], chunk(i, j, nv), 0))],
            out_specs=pl.BlockSpec((tm, D_MODEL), lambda i, j, te, nv, tok: (i, 0)),
            scratch_shapes=[pltpu.VMEM((tm, D_MODEL), F32), pltpu.VMEM((tm, D_MODEL), BF16),
                            pltpu.VMEM((tm, D_MODEL), F32), pltpu.SemaphoreType.DMA(())]),
        out_shape=jax.ShapeDtypeStruct(((n_tiles + 1) * tm, D_MODEL), F32),
        compiler_params=_params("arbitrary", "arbitrary"),
        name="moe_experts",
    )(tile_expert, n_used, tok_sorted, h2, w_sorted, wg, wu, wd)

    tc = TOKEN_TILE // 2
    mrow = lambda i: mod_row(i * tc // TOKEN_TILE)
    return pl.pallas_call(
        _combine_kernel,
        grid_spec=pltpu.PrefetchScalarGridSpec(
            num_scalar_prefetch=2,
            grid=(t // tc,),
            in_specs=[pl.BlockSpec(memory_space=pl.ANY),
                      pl.BlockSpec((tc, D_MODEL), lambda i, p0, p1: (i, 0)),
                      pl.BlockSpec((1, 6, D_MODEL), lambda i, p0, p1: (mrow(i), 0, 0)),
                      pl.BlockSpec((1, D_MODEL), lambda i, p0, p1: (0, 0))],
            out_specs=pl.BlockSpec((tc, D_MODEL), lambda i, p0, p1: (i, 0)),
            scratch_shapes=[pltpu.VMEM((tc, D_MODEL), F32), pltpu.VMEM((tc, D_MODEL), F32),
                            pltpu.SemaphoreType.DMA(()), pltpu.SemaphoreType.DMA(())]),
        out_shape=jax.ShapeDtypeStruct((t, D_MODEL), F32),
        compiler_params=_params("arbitrary"),
        name="moe_combine",
    )(p0, p1, y, x, mod, g_post)


def _reorder_w_in(w):
    gate0 = _OFF_BO + B_W
    body = jnp.concatenate([w[:, :gate0], w[:, gate0 + N_GATES:]], axis=1)
    gates = w[:, gate0:gate0 + N_GATES]
    pad = jnp.zeros((w.shape[0], LANES - N_GATES), w.dtype)
    return jnp.concatenate([body, gates, pad], axis=1).astype(BF16)


def _rope_tables(n):
    tok = jnp.arange(n, dtype=jnp.int32)
    pos = jnp.stack([tok // GRID_W, tok % GRID_W], axis=-1).astype(F32)
    inv = ROPE_THETA ** (-jnp.arange(ROPE_QUARTER, dtype=F32) / ROPE_QUARTER)
    ang = pos[:, :, None] * inv
    cos, sin = jnp.cos(ang), jnp.sin(ang)
    cos_t = jnp.concatenate([cos[:, 0], cos[:, 0], cos[:, 1], cos[:, 1]], axis=-1)
    sin_t = jnp.concatenate([-sin[:, 0], sin[:, 0], -sin[:, 1], sin[:, 1]], axis=-1)
    return cos_t, sin_t


def _head_major(cache):
    b, l, h, d = cache.shape
    return jnp.transpose(cache, (2, 0, 1, 3)).reshape(h, b * l, d).astype(BF16)


def kernel(x_prompt, x_sample, cache_gqa_k, cache_gqa_v, cache_na_k, cache_na_v, state_mlstm_C, state_mlstm_n, state_mlstm_m, c, c_ctx, w_mod, b_mod, g_pre_mix, g_post_mix, g_pre_ffn, g_post_ffn, w_in, w_out, g_q, g_k, mlstm_gate_bias, g_mlstm_out, na_rpb, w_ffn_gate, w_ffn_up, w_ffn_down, w_router, w_exp_gate, w_exp_up, w_exp_down):
    batch, seq, _ = x_prompt.shape
    dec_batch, dec_seq, _ = x_sample.shape
    assert seq % TOKEN_TILE == 0 or TOKEN_TILE % seq == 0
    assert dec_seq % (2 * TOKEN_TILE) == 0 and dec_seq % (NA_Q_ROWS * GRID_W) == 0
    tp, ts = batch * seq, dec_batch * dec_seq
    xp = x_prompt.reshape(tp, D_MODEL)
    xs = x_sample.reshape(ts, D_MODEL)

    n_mod = 8
    cvecs = jnp.concatenate([c_ctx[None, :], c, jnp.zeros((n_mod - 1 - dec_batch, D_MODEL), F32)], axis=0)
    mod_all = _modulation(cvecs, w_mod, b_mod)
    prompt_row = lambda i: 0
    tiles_per_sample = dec_seq // TOKEN_TILE
    sample_row = lambda i: 1 + i // tiles_per_sample

    rope_tabs = _rope_tables(dec_seq)
    row2 = lambda v: v.reshape(1, -1)
    ns = 2 * B_HEADS
    zeros_c = jnp.zeros((batch, ns, HEAD_DIM, LANES), F32)
    zeros_m = jnp.zeros((batch, ns, LANES), F32)

    ka_l, va_l, kc_l, vc_l, cst_l, m_l = [], [], [], [], [], []
    for l in range(DEPTH):
        mod = mod_all[l]
        w_in_l = _reorder_w_in(w_in[l])
        w_out_l = w_out[l].astype(BF16)
        gbias = jnp.concatenate([mlstm_gate_bias[l].reshape(1, N_GATES),
                                 jnp.zeros((1, LANES - N_GATES), F32)], axis=1)
        gout = g_mlstm_out[l].reshape(B_HEADS, 1, HEAD_DIM)
        moe = l % 2 == 1
        if moe:
            w_r = jnp.concatenate([w_router[l // 2], jnp.zeros((D_MODEL, LANES - N_EXPERTS), F32)], axis=1)
            ffn_w = tuple(w[l // 2].astype(BF16) for w in (w_exp_gate, w_exp_up, w_exp_down))
        else:
            w_r = None
            ffn_w = tuple(w[l // 2].astype(BF16) for w in (w_ffn_gate, w_ffn_up, w_ffn_down))

        def channel_mixer(res, x_mid, mod_row):
            if moe:
                return _moe_ffn(res[1], res[2], *ffn_w, x_mid, mod, mod_row, row2(g_post_ffn[l]))
            return _dense_ffn(res[1], *ffn_w, x_mid, mod, mod_row, row2(g_post_ffn[l]))

        (qa, ka, va, qb, kb, vb, ob, gb, qc, kc, vc, cka, cva, ckc, cvc) = _in_projection(
            xp, mod, prompt_row, row2(g_pre_mix[l]), w_in_l, row2(g_q[l]), row2(g_k[l]), None, True)
        a_out = _attention(qa, ka, va, seq, seq)
        b_out, cst, mst = _mlstm(qb, kb, vb, ob, gb, gbias, gout, zeros_c, zeros_m, seq)
        c_out = _attention(qc, kc, vc, seq, seq)
        res = _out_projection(a_out, b_out, c_out, w_out_l, xp, mod, prompt_row,
                              row2(g_post_mix[l]), row2(g_pre_ffn[l]), w_r)
        xp = channel_mixer(res, res[0], prompt_row)
        ka_l.append(cka.reshape(batch, seq, A_KV_HEADS, HEAD_DIM))
        va_l.append(cva.reshape(batch, seq, A_KV_HEADS, HEAD_DIM))
        kc_l.append(ckc.reshape(batch, seq, C_HEADS, HEAD_DIM))
        vc_l.append(cvc.reshape(batch, seq, C_HEADS, HEAD_DIM))
        cst_l.append(cst.reshape(batch, 2, B_HEADS, HEAD_DIM, LANES))
        m_l.append(mst.reshape(batch, 2, B_HEADS, LANES)[..., 0])

        (qa, ka, va, qb, kb, vb, ob, gb, qc, kc, vc) = _in_projection(
            xs, mod, sample_row, row2(g_pre_mix[l]), w_in_l, row2(g_q[l]), row2(g_k[l]), rope_tabs, False)
        a_out = _attention(qa, ka, va, dec_seq, 256,
                           ctx=(_head_major(cache_gqa_k[:, l]), _head_major(cache_gqa_v[:, l])))
        c0 = jnp.concatenate([state_mlstm_C[:, l], state_mlstm_n[:, l][..., None],
                              jnp.zeros((dec_batch, 2, B_HEADS, HEAD_DIM, LANES - HEAD_DIM - 1), F32)],
                             axis=-1).reshape(dec_batch, ns, HEAD_DIM, LANES)
        m0 = jnp.broadcast_to(state_mlstm_m[:, l].reshape(dec_batch, ns, 1), (dec_batch, ns, LANES))
        b_out, _, _ = _mlstm(qb, kb, vb, ob, gb, gbias, gout, c0, m0, dec_seq)
        c_out = _neighbourhood_attention(qc, kc, vc, _head_major(cache_na_k[:, l]),
                                         _head_major(cache_na_v[:, l]),
                                         _na_toeplitz(na_rpb[l]), dec_seq)
        res = _out_projection(a_out, b_out, c_out, w_out_l, xs, mod, sample_row,
                              row2(g_post_mix[l]), row2(g_pre_ffn[l]), w_r)
        xs = channel_mixer(res, res[0], sample_row)

    cst = jnp.stack(cst_l, axis=1)
    return (xp.reshape(batch, seq, D_MODEL), xs.reshape(dec_batch, dec_seq, D_MODEL),
            jnp.stack(ka_l, axis=1), jnp.stack(va_l, axis=1),
            jnp.stack(kc_l, axis=1), jnp.stack(vc_l, axis=1),
            cst[..., :HEAD_DIM], cst[..., HEAD_DIM], jnp.stack(m_l, axis=1))
```

```python
import functools

import jax
import jax.numpy as jnp
from jax import lax
from jax.experimental import pallas as pl
from jax.experimental.pallas import tpu as pltpu

D_MODEL = 1024
DEPTH = 2
GRID_W = 64
HEAD_DIM = 64
A_HEADS = 6
A_KV_HEADS = 2
A_GROUP = A_HEADS // A_KV_HEADS
B_HEADS = 4
C_HEADS = 6
A_Q_W = A_HEADS * HEAD_DIM
A_KV_W = A_KV_HEADS * HEAD_DIM
B_W = B_HEADS * HEAD_DIM
C_W = C_HEADS * HEAD_DIM
MLSTM_CHUNK = 64
NA_WIN_R = 8
NA_WIN_C = 16
ROPE_THETA = 10000.0
ROPE_QUARTER = HEAD_DIM // 4
ATTN_SCALE = HEAD_DIM ** -0.5
N_EXPERTS = 8
EPS = 1e-6

LANES = 128
N_GATES = 4 * B_HEADS
MASKED = -1e30

_OFF_AQ = 0
_OFF_AK = _OFF_AQ + A_Q_W
_OFF_AV = _OFF_AK + A_KV_W
_OFF_BQ = _OFF_AV + A_KV_W
_OFF_BK = _OFF_BQ + B_W
_OFF_BV = _OFF_BK + B_W
_OFF_BO = _OFF_BV + B_W
_OFF_CQ = _OFF_BO + B_W
_OFF_CK = _OFF_CQ + C_W
_OFF_CV = _OFF_CK + C_W
_OFF_BG = _OFF_CV + C_W
P_IN_PAD = _OFF_BG + LANES

TOKEN_TILE = 512
NA_Q_ROWS = 8
NA_K_ROWS = 16
MOE_FF_TILE = 896
ATTN_ROW_PARTS = 2
MLSTM_STEP_TOKENS = 1024
VMEM_LIMIT = 56 * 1024 * 1024

F32 = jnp.float32
BF16 = jnp.bfloat16
HIGHEST = lax.Precision.HIGHEST
_NT = (((1,), (1,)), ((), ()))


def _params(*sem):
    return pltpu.CompilerParams(dimension_semantics=sem, vmem_limit_bytes=VMEM_LIMIT)


def _rms(x, g):
    return x * lax.rsqrt(jnp.mean(x * x, axis=-1, keepdims=True) + EPS) * g


def _silu(x):
    return x * jax.nn.sigmoid(x)


def _mod_kernel(c_ref, w_ref, b_ref, o_ref):
    s = _silu(c_ref[...])
    o_ref[0] = jnp.dot(s, w_ref[0], precision=HIGHEST, preferred_element_type=F32) + b_ref[0]


def _modulation(cvecs, w_mod, b_mod):
    nb = cvecs.shape[0]
    tn = 1536
    out = pl.pallas_call(
        _mod_kernel,
        grid=(DEPTH, 6 * D_MODEL // tn),
        in_specs=[pl.BlockSpec((nb, D_MODEL), lambda l, j: (0, 0)),
                  pl.BlockSpec((1, D_MODEL, tn), lambda l, j: (l, 0, j)),
                  pl.BlockSpec((1, 1, tn), lambda l, j: (l, 0, j))],
        out_specs=pl.BlockSpec((1, nb, tn), lambda l, j: (l, 0, j)),
        out_shape=jax.ShapeDtypeStruct((DEPTH, nb, 6 * D_MODEL), F32),
        compiler_params=_params("parallel", "parallel"),
        name="adaln_mod",
    )(cvecs, w_mod, b_mod.reshape(DEPTH, 1, 6 * D_MODEL))
    return out.reshape(DEPTH, nb, 6, D_MODEL)


def _inproj_kernel(*refs, rope, caches):
    x_ref, mod_ref, g_ref, w_ref, gq_ref, gk_ref = refs[:6]
    pos = 6
    if rope:
        cos_ref, sin_ref = refs[pos:pos + 2]
        pos += 2
    (qa_ref, ka_ref, va_ref, qb_ref, kb_ref, vb_ref, ob_ref, gb_ref,
     qc_ref, kc_ref, vc_ref) = refs[pos:pos + 11]
    pos += 11
    if caches:
        cka_ref, cva_ref, ckc_ref, cvc_ref = refs[pos:pos + 4]

    m = mod_ref[0]
    h = _rms(x_ref[...], g_ref[...]) * (1.0 + m[1:2]) + m[0:1]
    y = jnp.dot(h.astype(BF16), w_ref[...], preferred_element_type=F32)

    def head(off, j):
        return y[:, off + j * HEAD_DIM: off + (j + 1) * HEAD_DIM]

    def rotary(p):
        if not rope:
            return p
        swapped = jnp.concatenate([p[:, 16:32], p[:, 0:16], p[:, 48:64], p[:, 32:48]], axis=1)
        return p * cos_ref[...] + swapped * sin_ref[...]

    for j in range(A_HEADS):
        qa_ref[j] = (rotary(_rms(head(_OFF_AQ, j), gq_ref[...])) * ATTN_SCALE).astype(BF16)
    for j in range(A_KV_HEADS):
        kn = _rms(head(_OFF_AK, j), gk_ref[...])
        ka_ref[j] = rotary(kn).astype(BF16)
        va_ref[j] = head(_OFF_AV, j).astype(BF16)
        if caches:
            cka_ref[:, j * HEAD_DIM:(j + 1) * HEAD_DIM] = kn
    lane = lax.broadcasted_iota(jnp.int32, (y.shape[0], LANES), 1)
    ones_col = jnp.where(lane == HEAD_DIM, 1.0, 0.0)
    for j in range(B_HEADS):
        qb_ref[j] = (head(_OFF_BQ, j) * ATTN_SCALE).astype(BF16)
        kb_ref[j] = head(_OFF_BK, j).astype(BF16)
        wide = y[:, _OFF_BV + j * HEAD_DIM: _OFF_BV + j * HEAD_DIM + LANES]
        vb_ref[j] = jnp.where(lane < HEAD_DIM, wide, ones_col).astype(BF16)
    ob_ref[...] = y[:, _OFF_BO:_OFF_BO + B_W]
    gb_ref[...] = y[:, _OFF_BG:_OFF_BG + LANES]
    for j in range(C_HEADS):
        qc_ref[j] = (head(_OFF_CQ, j) * ATTN_SCALE).astype(BF16)
        kc_ref[j] = head(_OFF_CK, j).astype(BF16)
        vc_ref[j] = head(_OFF_CV, j).astype(BF16)
    if caches:
        cva_ref[...] = y[:, _OFF_AV:_OFF_AV + A_KV_W]
        ckc_ref[...] = y[:, _OFF_CK:_OFF_CK + C_W]
        cvc_ref[...] = y[:, _OFF_CV:_OFF_CV + C_W]


def _in_projection(x, mod, mod_row, g_pre, w_in, g_q, g_k, rope_tabs, caches):
    t = x.shape[0]
    tm = TOKEN_TILE
    rope = rope_tabs is not None
    row = lambda i: (i, 0)
    heads = lambda i: (0, i, 0)
    fixed2 = lambda i: (0, 0)
    in_specs = [pl.BlockSpec((tm, D_MODEL), row),
                pl.BlockSpec((1, 6, D_MODEL), lambda i: (mod_row(i), 0, 0)),
                pl.BlockSpec((1, D_MODEL), fixed2),
                pl.BlockSpec((D_MODEL, P_IN_PAD), fixed2),
                pl.BlockSpec((1, HEAD_DIM), fixed2),
                pl.BlockSpec((1, HEAD_DIM), fixed2)]
    args = [x, mod, g_pre, w_in, g_q, g_k]
    if rope:
        n_pos = rope_tabs[0].shape[0] // tm
        in_specs += [pl.BlockSpec((tm, HEAD_DIM), lambda i: (i % n_pos, 0))] * 2
        args += list(rope_tabs)

    def hm(nh, width=HEAD_DIM):
        return (jax.ShapeDtypeStruct((nh, t, width), BF16), pl.BlockSpec((nh, tm, width), heads))

    def tokmajor(width):
        return (jax.ShapeDtypeStruct((t, width), F32), pl.BlockSpec((tm, width), row))

    outs = [hm(A_HEADS), hm(A_KV_HEADS), hm(A_KV_HEADS),
            hm(B_HEADS), hm(B_HEADS), hm(B_HEADS, LANES), tokmajor(B_W), tokmajor(LANES),
            hm(C_HEADS), hm(C_HEADS), hm(C_HEADS)]
    if caches:
        outs += [tokmajor(A_KV_W), tokmajor(A_KV_W), tokmajor(C_W), tokmajor(C_W)]
    return pl.pallas_call(
        functools.partial(_inproj_kernel, rope=rope, caches=caches),
        grid=(t // tm,),
        in_specs=in_specs,
        out_specs=[o[1] for o in outs],
        out_shape=[o[0] for o in outs],
        compiler_params=_params("parallel"),
        name="in_projection",
    )(*args)


def _softmax_attend(q, keys, values, bias_ref):
    step = q.shape[0] // ATTN_ROW_PARTS
    scores = []
    for part in range(ATTN_ROW_PARTS):
        qp = q[part * step:(part + 1) * step]
        sc = [lax.dot_general(qp, kb, _NT, preferred_element_type=F32) for kb in keys]
        if bias_ref is not None:
            sc[0] = sc[0] + bias_ref[part * step:(part + 1) * step, :]
        scores.append(sc)
    probs = []
    for sc in scores:
        mx = functools.reduce(jnp.maximum, [jnp.max(s, axis=-1, keepdims=True) for s in sc])
        ps = [jnp.exp(s - mx) for s in sc]
        den = functools.reduce(lambda a, b: a + b, [jnp.sum(p, axis=-1, keepdims=True) for p in ps])
        probs.append((ps, den))
    outs = []
    for ps, den in probs:
        o = functools.reduce(lambda a, b: a + b,
                             [jnp.dot(p.astype(BF16), vb, preferred_element_type=F32)
                              for p, vb in zip(ps, values)])
        outs.append(o / den)
    return jnp.concatenate(outs, axis=0)


def _attn_kernel(*refs, ctx):
    if ctx:
        q_ref, k_ref, v_ref, kx_ref, vx_ref, o_ref = refs
    else:
        q_ref, k_ref, v_ref, o_ref = refs
    g, tq, d = q_ref.shape
    q = q_ref[...].reshape(g * tq, d)
    keys, values = [k_ref[0]], [v_ref[0]]
    if ctx:
        keys.append(kx_ref[0])
        values.append(vx_ref[0])
    o_ref[...] = _softmax_attend(q, keys, values, None).reshape(g, tq, d)


def _attention(q, k, v, n_seq, tq, ctx=None):
    hq, t, d = q.shape
    hkv = k.shape[0]
    g = hq // hkv
    nb = t // n_seq
    nq = n_seq // tq
    in_specs = [pl.BlockSpec((g, tq, d), lambda b, kv, i: (kv, b * nq + i, 0)),
                pl.BlockSpec((1, n_seq, d), lambda b, kv, i: (kv, b, 0)),
                pl.BlockSpec((1, n_seq, d), lambda b, kv, i: (kv, b, 0))]
    args = [q, k, v]
    if ctx is not None:
        lc = ctx[0].shape[1] // nb
        in_specs += [pl.BlockSpec((1, lc, d), lambda b, kv, i: (kv, b, 0))] * 2
        args += list(ctx)
    return pl.pallas_call(
        functools.partial(_attn_kernel, ctx=ctx is not None),
        grid=(nb, hkv, nq),
        in_specs=in_specs,
        out_specs=pl.BlockSpec((g, tq, d), lambda b, kv, i: (kv, b * nq + i, 0)),
        out_shape=jax.ShapeDtypeStruct((hq, t, d), F32),
        compiler_params=_params("parallel", "parallel", "parallel"),
        name="dense_attention",
    )(*args)


def _na_kernel(q_ref, k_ref, v_ref, kx_ref, vx_ref, toep_ref, o_ref, bias_scr):
    i = pl.program_id(1)
    nk = bias_scr.shape[1]
    rows = k_ref.shape[1] // GRID_W
    first_row = jnp.clip(NA_Q_ROWS * i - NA_WIN_R // 2, 0, rows - NA_K_ROWS)

    @pl.when(pl.program_id(2) == 0)
    def _():
        for a in range(NA_Q_ROWS):
            qr = NA_Q_ROWS * i + a
            r_start = jnp.clip(qr - NA_WIN_R // 2, 0, rows - NA_WIN_R)
            for w in range(NA_K_ROWS):
                kr = first_row + w
                in_window = (kr >= r_start) & (kr < r_start + NA_WIN_R)
                dr = jnp.clip(kr - qr + NA_WIN_R - 1, 0, 2 * NA_WIN_R - 2)
                bias_scr[a * GRID_W:(a + 1) * GRID_W, w * GRID_W:(w + 1) * GRID_W] = (
                    toep_ref[0, dr] + jnp.where(in_window, 0.0, MASKED))

    start = pl.multiple_of(first_row * GRID_W, GRID_W)
    q = q_ref[0]
    k = k_ref[0, pl.ds(start, nk), :]
    v = v_ref[0, pl.ds(start, nk), :]
    o_ref[0] = _softmax_attend(q, [k, kx_ref[0]], [v, vx_ref[0]], bias_scr)


def _na_toeplitz(rpb):
    qc = jnp.arange(GRID_W, dtype=jnp.int32)[:, None]
    kc = jnp.arange(GRID_W, dtype=jnp.int32)[None, :]
    c_start = jnp.clip(qc - NA_WIN_C // 2, 0, GRID_W - NA_WIN_C)
    in_window = (kc >= c_start) & (kc < c_start + NA_WIN_C)
    offs = jnp.arange(2 * NA_WIN_C - 1, dtype=jnp.int32)[:, None, None]
    onehot = ((kc - qc + NA_WIN_C - 1)[None] == offs) & in_window[None]
    table = jnp.einsum('hrb,bqk->hrqk', rpb.astype(F32), onehot.astype(F32), precision=HIGHEST)
    return table + jnp.where(in_window, 0.0, MASKED)


def _neighbourhood_attention(q, k, v, kx, vx, toep, n_seq):
    h, t, d = q.shape
    nb = t // n_seq
    tq = NA_Q_ROWS * GRID_W
    nblk = n_seq // tq
    lc = kx.shape[1] // nb
    return pl.pallas_call(
        _na_kernel,
        grid=(h, nblk, nb),
        in_specs=[pl.BlockSpec((1, tq, d), lambda hh, i, b: (hh, b * nblk + i, 0)),
                  pl.BlockSpec((1, n_seq, d), lambda hh, i, b: (hh, b, 0)),
                  pl.BlockSpec((1, n_seq, d), lambda hh, i, b: (hh, b, 0)),
                  pl.BlockSpec((1, lc, d), lambda hh, i, b: (hh, b, 0)),
                  pl.BlockSpec((1, lc, d), lambda hh, i, b: (hh, b, 0)),
                  pl.BlockSpec((1,) + toep.shape[1:], lambda hh, i, b: (hh, 0, 0, 0))],
        out_specs=pl.BlockSpec((1, tq, d), lambda hh, i, b: (hh, b * nblk + i, 0)),
        out_shape=jax.ShapeDtypeStruct((h, t, d), F32),
        scratch_shapes=[pltpu.VMEM((tq, NA_K_ROWS * GRID_W), F32)],
        compiler_params=_params("parallel", "parallel", "arbitrary"),
        name="neighbourhood_attention",
    )(q, k, v, kx, vx, toep)


def _mlstm_kernel(q_ref, k_ref, v_ref, o_ref, g_ref, gbias_ref, gout_ref, c0_ref, m0_ref,
                  h_ref, cn_ref, mn_ref, hs_fwd, hs_bwd):
    L = MLSTM_CHUNK
    nseq, ns = m0_ref.shape[:2]
    n = q_ref.shape[1] // nseq
    nc = n // L
    hs_dir = (hs_fwd, hs_bwd)

    row = lax.broadcasted_iota(jnp.int32, (L, L), 0)
    col = lax.broadcasted_iota(jnp.int32, (L, L), 1)
    masks = (col <= row, col >= row)
    cum_mats = tuple(mk.astype(F32) for mk in masks)
    lane = lax.broadcasted_iota(jnp.int32, (L, LANES), 1)
    eye_l = (lax.broadcasted_iota(jnp.int32, (LANES, LANES), 0)
             == lax.broadcasted_iota(jnp.int32, (LANES, LANES), 1)).astype(F32)
    eye_k = (row == col).astype(BF16)

    def column(xmat, j):
        return jnp.sum(jnp.where(lane == j, xmat, 0.0), axis=1, keepdims=True)

    def body(i, carry):
        cs, ms = list(carry[0]), list(carry[1])
        groups = [(sq, d) for sq in range(nseq) for d in range(2)]
        streams = [(sq, d, hh) for sq, d in groups for hh in range(B_HEADS)]
        rows_of, pre_of, cum_of, pre_t_of, cum_t_of = {}, {}, {}, {}, {}
        for sq, d in groups:
            c = i if d == 0 else nc - 1 - i
            rows_of[sq, d] = pl.ds(pl.multiple_of(sq * n + c * L, L), L)
            pre = g_ref[rows_of[sq, d], :] + gbias_ref[...]
            logf = jnp.minimum(pre, 0.0) - jnp.log1p(jnp.exp(-jnp.abs(pre)))
            pre_of[sq, d] = pre
            cum_of[sq, d] = jnp.dot(cum_mats[d], logf, precision=HIGHEST, preferred_element_type=F32)
        for key in groups:
            pre_t_of[key] = lax.dot_general(eye_l, pre_of[key], _NT, precision=HIGHEST,
                                            preferred_element_type=F32)
            cum_t_of[key] = lax.dot_general(eye_l, cum_of[key], _NT, precision=HIGHEST,
                                            preferred_element_type=F32)

        idx_of = {st: st[0] * ns + st[1] * B_HEADS + st[2] for st in streams}
        qkv, qk_of, qc_of, kt_of = {}, {}, {}, {}
        for st in streams:
            sq, d, hh = st
            rows = rows_of[sq, d]
            qc, kc, vc = q_ref[hh, rows, :], k_ref[hh, rows, :], v_ref[hh, rows, :]
            qkv[st] = (qc, kc, vc)
            qk_of[st] = lax.dot_general(qc, kc, _NT, preferred_element_type=F32)
            kt_of[st] = lax.dot_general(eye_k, kc, _NT, preferred_element_type=F32).astype(BF16)
        for st in streams:
            qc_of[st] = jnp.dot(qkv[st][0], cs[idx_of[st]].astype(BF16), preferred_element_type=F32)

        gate_of = {}
        for st in streams:
            sq, d, hh = st
            ji = d * 2 * B_HEADS + hh
            jf = ji + B_HEADS
            b_col = column(cum_of[sq, d], jf)
            li_col = column(pre_of[sq, d], ji)
            b_row = cum_t_of[sq, d][jf:jf + 1, :]
            li_row = pre_t_of[sq, d][ji:ji + 1, :]
            m_prev = ms[idx_of[st]][:, 0:1]
            log_d = jnp.where(masks[d], b_col - b_row + li_row, -jnp.inf)
            m_inter = b_col + m_prev
            m_t = jnp.maximum(m_inter, jnp.max(log_d, axis=-1, keepdims=True))
            gate_of[st] = (b_col, li_col, m_prev, m_t, jnp.exp(m_inter - m_t), jnp.exp(log_d - m_t))

        num_of = {}
        for st in streams:
            w_inter, decay_mat = gate_of[st][4], gate_of[st][5]
            s = qk_of[st] * decay_mat
            num_of[st] = (jnp.dot(s.astype(BF16), qkv[st][2], preferred_element_type=F32)
                          + w_inter * qc_of[st])

        for st in streams:
            sq, d, hh = st
            b_col, li_col, m_prev, m_t = gate_of[st][:4]
            num = num_of[st]
            den = column(num, HEAD_DIM)
            hs_dir[d][hh, rows_of[sq, d], :] = (num[:, :HEAD_DIM]
                                               / jnp.maximum(jnp.abs(den), jnp.exp(-m_t)))
            last = L - 1 if d == 0 else 0
            m_new = m_t[last:last + 1, :]
            b_last = b_col[last:last + 1, :]
            w = jnp.exp(b_last - b_col + li_col - m_new)
            decay = jnp.exp(b_last + m_prev - m_new)
            wv = (w * qkv[st][2].astype(F32)).astype(BF16)
            cs[idx_of[st]] = decay * cs[idx_of[st]] + jnp.dot(kt_of[st], wv, preferred_element_type=F32)
            ms[idx_of[st]] = jnp.broadcast_to(m_new, (1, LANES))
        return tuple(cs), tuple(ms)

    streams = [(sq, s) for sq in range(nseq) for s in range(ns)]
    cs, ms = lax.fori_loop(0, nc, body, (tuple(c0_ref[sq, s] for sq, s in streams),
                                         tuple(m0_ref[sq, s:s + 1, :] for sq, s in streams)))

    for hh in range(B_HEADS):
        gate = jax.nn.sigmoid(o_ref[:, hh * HEAD_DIM:(hh + 1) * HEAD_DIM])
        h_ref[hh] = _rms(hs_fwd[hh] + hs_bwd[hh], gout_ref[hh]) * gate
    for idx, (sq, s) in enumerate(streams):
        cn_ref[sq, s] = cs[idx]
        mn_ref[sq, s:s + 1, :] = ms[idx]


def _mlstm(q, k, v, o, g, gate_bias, g_out, c0, m0, n_seq):
    h, t, d = q.shape
    nsq = max(1, MLSTM_STEP_TOKENS // n_seq)
    nb = t // n_seq
    assert nb % nsq == 0
    ns = 2 * h
    heads = lambda b: (0, b, 0)
    return pl.pallas_call(
        _mlstm_kernel,
        grid=(nb // nsq,),
        in_specs=[pl.BlockSpec((h, nsq * n_seq, d), heads),
                  pl.BlockSpec((h, nsq * n_seq, d), heads),
                  pl.BlockSpec((h, nsq * n_seq, LANES), heads),
                  pl.BlockSpec((nsq * n_seq, h * d), lambda b: (b, 0)),
                  pl.BlockSpec((nsq * n_seq, LANES), lambda b: (b, 0)),
                  pl.BlockSpec((1, LANES), lambda b: (0, 0)),
                  pl.BlockSpec((h, 1, d), lambda b: (0, 0, 0)),
                  pl.BlockSpec((nsq, ns, d, LANES), lambda b: (b, 0, 0, 0)),
                  pl.BlockSpec((nsq, ns, LANES), lambda b: (b, 0, 0))],
        out_specs=[pl.BlockSpec((h, nsq * n_seq, d), heads),
                   pl.BlockSpec((nsq, ns, d, LANES), lambda b: (b, 0, 0, 0)),
                   pl.BlockSpec((nsq, ns, LANES), lambda b: (b, 0, 0))],
        out_shape=[jax.ShapeDtypeStruct((h, t, d), F32),
                   jax.ShapeDtypeStruct((nb, ns, d, LANES), F32),
                   jax.ShapeDtypeStruct((nb, ns, LANES), F32)],
        scratch_shapes=[pltpu.VMEM((h, nsq * n_seq, d), F32), pltpu.VMEM((h, nsq * n_seq, d), F32)],
        compiler_params=_params("parallel"),
        name="mlstm",
    )(q, k, v, o, g, gate_bias, g_out, c0, m0)


def _outproj_kernel(*refs, moe):
    a_ref, b_ref, c_ref, w_ref, x_ref, mod_ref, gpost_ref, gpre_ref = refs[:8]
    pos = 8
    if moe:
        wr_ref = refs[pos]
        pos += 1
    xo_ref, h2_ref = refs[pos:pos + 2]
    pos += 2
    if moe:
        gates_ref = refs[pos]
        pos += 1
    cat_ref = refs[pos]

    off = 0
    for src in (a_ref, b_ref, c_ref):
        for j in range(src.shape[0]):
            cat_ref[:, off:off + HEAD_DIM] = src[j]
            off += HEAD_DIM
    out = jnp.dot(cat_ref[...].astype(BF16), w_ref[...], preferred_element_type=F32)
    m = mod_ref[0]
    xn = x_ref[...] + m[2:3] * _rms(out, gpost_ref[...])
    xo_ref[...] = xn
    h2 = _rms(xn, gpre_ref[...]) * (1.0 + m[4:5]) + m[3:4]
    h2_ref[...] = h2.astype(h2_ref.dtype)
    if moe:
        logits = jnp.dot(h2, wr_ref[...], precision=HIGHEST, preferred_element_type=F32)
        lane = lax.broadcasted_iota(jnp.int32, logits.shape, 1)
        logits = jnp.where(lane < N_EXPERTS, logits, -jnp.inf)
        v1 = jnp.max(logits, axis=-1, keepdims=True)
        i1 = jnp.min(jnp.where(logits == v1, lane, LANES), axis=-1, keepdims=True)
        rest = jnp.where(lane == i1, -jnp.inf, logits)
        v2 = jnp.max(rest, axis=-1, keepdims=True)
        i2 = jnp.min(jnp.where(rest == v2, lane, LANES), axis=-1, keepdims=True)
        e2 = jnp.exp(v2 - v1)
        w1 = 1.0 / (1.0 + e2)
        w2 = e2 / (1.0 + e2)
        gates_ref[...] = jnp.where(lane == i1, w1, 0.0) + jnp.where(lane == i2, w2, 0.0)


def _out_projection(a, b, c, w_out, x, mod, mod_row, g_post, g_pre, w_router):
    t = x.shape[0]
    tm = TOKEN_TILE
    moe = w_router is not None
    row = lambda i: (i, 0)
    fixed2 = lambda i: (0, 0)
    heads = lambda i: (0, i, 0)
    in_specs = [pl.BlockSpec((a.shape[0], tm, HEAD_DIM), heads),
                pl.BlockSpec((b.shape[0], tm, HEAD_DIM), heads),
                pl.BlockSpec((c.shape[0], tm, HEAD_DIM), heads),
                pl.BlockSpec((D_MODEL, D_MODEL), fixed2),
                pl.BlockSpec((tm, D_MODEL), row),
                pl.BlockSpec((1, 6, D_MODEL), lambda i: (mod_row(i), 0, 0)),
                pl.BlockSpec((1, D_MODEL), fixed2),
                pl.BlockSpec((1, D_MODEL), fixed2)]
    args = [a, b, c, w_out, x, mod, g_post, g_pre]
    out_specs = [pl.BlockSpec((tm, D_MODEL), row), pl.BlockSpec((tm, D_MODEL), row)]
    out_shape = [jax.ShapeDtypeStruct((t, D_MODEL), F32),
                 jax.ShapeDtypeStruct((t, D_MODEL), F32 if moe else BF16)]
    if moe:
        in_specs.append(pl.BlockSpec((D_MODEL, LANES), fixed2))
        args.append(w_router)
        out_specs.append(pl.BlockSpec((tm, LANES), row))
        out_shape.append(jax.ShapeDtypeStruct((t, LANES), F32))
    return pl.pallas_call(
        functools.partial(_outproj_kernel, moe=moe),
        grid=(t // tm,),
        in_specs=in_specs,
        out_specs=out_specs,
        out_shape=out_shape,
        scratch_shapes=[pltpu.VMEM((tm, D_MODEL), F32)],
        compiler_params=_params("parallel"),
        name="out_projection",
    )(*args)


def _swiglu_chunk(h, wg, wu, wd):
    a = jnp.dot(h, wg, preferred_element_type=F32)
    b = jnp.dot(h, wu, preferred_element_type=F32)
    return jnp.dot((_silu(a) * b).astype(BF16), wd, preferred_element_type=F32)


def _ffn_kernel(h_ref, wg_ref, wu_ref, wd_ref, x_ref, mod_ref, g_ref, o_ref, acc_ref):
    j = pl.program_id(1)

    @pl.when(j == 0)
    def _():
        acc_ref[...] = jnp.zeros(acc_ref.shape, F32)

    acc_ref[...] += _swiglu_chunk(h_ref[...], wg_ref[...], wu_ref[...], wd_ref[...])

    @pl.when(j == pl.num_programs(1) - 1)
    def _():
        o_ref[...] = x_ref[...] + mod_ref[0][5:6] * _rms(acc_ref[...], g_ref[...])


def _dense_ffn(h2, wg, wu, wd, x, mod, mod_row, g_post):
    t = x.shape[0]
    tm = TOKEN_TILE
    ff = wg.shape[1]
    tf = ff // 2
    return pl.pallas_call(
        _ffn_kernel,
        grid=(t // tm, ff // tf),
        in_specs=[pl.BlockSpec((tm, D_MODEL), lambda i, j: (i, 0)),
                  pl.BlockSpec((D_MODEL, tf), lambda i, j: (0, j)),
                  pl.BlockSpec((D_MODEL, tf), lambda i, j: (0, j)),
                  pl.BlockSpec((tf, D_MODEL), lambda i, j: (j, 0)),
                  pl.BlockSpec((tm, D_MODEL), lambda i, j: (i, 0)),
                  pl.BlockSpec((1, 6, D_MODEL), lambda i, j: (mod_row(i), 0, 0)),
                  pl.BlockSpec((1, D_MODEL), lambda i, j: (0, 0))],
        out_specs=pl.BlockSpec((tm, D_MODEL), lambda i, j: (i, 0)),
        out_shape=jax.ShapeDtypeStruct((t, D_MODEL), F32),
        scratch_shapes=[pltpu.VMEM((tm, D_MODEL), F32)],
        compiler_params=_params("parallel", "arbitrary"),
        name="dense_ffn",
    )(h2, wg, wu, wd, x, mod, g_post)


def _row_copy(src_hbm, src_row, dst, dst_row, sem):
    return pltpu.make_async_copy(src_hbm.at[pl.ds(src_row, 1)], dst.at[pl.ds(dst_row, 1)], sem)


def _gather_rows(src_hbm, idx_ref, base, dst, sem):
    n = dst.shape[0]

    def issue(r, carry):
        _row_copy(src_hbm, idx_ref[base + r], dst, r, sem).start()
        return carry

    def drain(r, carry):
        _row_copy(src_hbm, 0, dst, r, sem).wait()
        return carry

    lax.fori_loop(0, n, issue, 0, unroll=8)
    lax.fori_loop(0, n, drain, 0, unroll=8)


def _moe_kernel(te_ref, nv_ref, tok_ref, h_hbm, w_ref, wg_ref, wu_ref, wd_ref, y_ref,
                xbuf, xb16, acc_ref, sem):
    del te_ref
    i = pl.program_id(0)
    j = pl.program_id(1)
    used = i < nv_ref[0]

    @pl.when(used & (j == 0))
    def _():
        _gather_rows(h_hbm, tok_ref, i * xbuf.shape[0], xbuf, sem)
        xb16[...] = xbuf[...].astype(BF16)
        acc_ref[...] = jnp.zeros(acc_ref.shape, F32)

    @pl.when(used)
    def _():
        acc_ref[...] += _swiglu_chunk(xb16[...], wg_ref[0], wu_ref[0], wd_ref[0])

    last = j == pl.num_programs(1) - 1

    @pl.when(used & last)
    def _():
        y_ref[...] = w_ref[:, 0:1] * acc_ref[...]

    @pl.when(jnp.logical_not(used) & last)
    def _():
        y_ref[...] = jnp.zeros(y_ref.shape, F32)


def _combine_kernel(p0_ref, p1_ref, y_hbm, x_ref, mod_ref, g_ref, o_ref, buf0, buf1, sem0, sem1):
    base = pl.program_id(0) * buf0.shape[0]
    _gather_rows(y_hbm, p0_ref, base, buf0, sem0)
    _gather_rows(y_hbm, p1_ref, base, buf1, sem1)
    f = buf0[...] + buf1[...]
    o_ref[...] = x_ref[...] + mod_ref[0][5:6] * _rms(f, g_ref[...])


def _route(gates, tm, n_tiles):
    t = gates.shape[0]
    g = gates[:, :N_EXPERTS]
    mask = g != 0.0
    mi = mask.astype(jnp.int32)
    rank = jnp.cumsum(mi, axis=0) - mi
    tiles_e = (jnp.sum(mi, axis=0) + tm - 1) // tm
    tile_end = jnp.cumsum(tiles_e)
    n_used = tile_end[-1]
    zero_row = n_tiles * tm
    pos = (tile_end - tiles_e)[None, :] * tm + rank
    rows = zero_row + tm
    tile_expert = jnp.minimum(
        jnp.sum(jnp.arange(n_tiles + 1, dtype=jnp.int32)[:, None] >= tile_end[None, :], axis=1),
        N_EXPERTS - 1).astype(jnp.int32)
    order = jnp.cumsum(mi, axis=1)
    first, second = mask & (order == 1), mask & (order == 2)
    p0 = jnp.min(jnp.where(first, pos, zero_row), axis=1).astype(jnp.int32)
    p1 = jnp.min(jnp.where(second, pos, zero_row), axis=1).astype(jnp.int32)
    w0 = jnp.sum(jnp.where(first, g, 0.0), axis=1)
    w1 = jnp.sum(jnp.where(second, g, 0.0), axis=1)
    tok = jnp.arange(t, dtype=jnp.int32)
    dest = jnp.concatenate([p0, p1])
    tok_sorted = jnp.zeros((rows,), jnp.int32).at[dest].set(jnp.concatenate([tok, tok]))
    w_sorted = jnp.zeros((rows,), F32).at[dest].set(jnp.concatenate([w0, w1]))
    return (tok_sorted, jnp.broadcast_to(w_sorted[:, None], (rows, LANES)), tile_expert,
            n_used.reshape(1).astype(jnp.int32), p0, p1)


def _moe_ffn(h2, gates, wg, wu, wd, x, mod, mod_row, g_post):
    t = x.shape[0]
    tm = TOKEN_TILE
    ne, _, ff = wg.shape
    tf = MOE_FF_TILE
    assert ff % tf == 0
    nj = ff // tf
    n_tiles = 2 * t // tm + ne
    tok_sorted, w_sorted, tile_expert, n_used, p0, p1 = _route(gates, tm, n_tiles)

    def chunk(i, j, nv):
        return jnp.where(i < nv[0], j, nj - 1)

    y = pl.pallas_call(
        _moe_kernel,
        grid_spec=pltpu.PrefetchScalarGridSpec(
            num_scalar_prefetch=3,
            grid=(n_tiles + 1, nj),
            in_specs=[pl.BlockSpec(memory_space=pl.ANY),
                      pl.BlockSpec((tm, LANES), lambda i, j, te, nv, tok: (i, 0)),
                      pl.BlockSpec((1, D_MODEL, tf), lambda i, j, te, nv, tok: (te[i], 0, chunk(i, j, nv))),
                      pl.BlockSpec((1, D_MODEL, tf), lambda i, j, te, nv, tok: (te[i], 0, chunk(i, j, nv))),
                      pl.BlockSpec((1, tf, D_MODEL), lambda i, j, te, nv, tok: (te[i], chunk(i, j, nv), 0))],
            out_specs=pl.BlockSpec((tm, D_MODEL), lambda i, j, te, nv, tok: (i, 0)),
            scratch_shapes=[pltpu.VMEM((tm, D_MODEL), F32), pltpu.VMEM((tm, D_MODEL), BF16),
                            pltpu.VMEM((tm, D_MODEL), F32), pltpu.SemaphoreType.DMA(())]),
        out_shape=jax.ShapeDtypeStruct(((n_tiles + 1) * tm, D_MODEL), F32),
        compiler_params=_params("arbitrary", "arbitrary"),
        name="moe_experts",
    )(tile_expert, n_used, tok_sorted, h2, w_sorted, wg, wu, wd)

    tc = TOKEN_TILE // 2
    mrow = lambda i: mod_row(i * tc // TOKEN_TILE)
    return pl.pallas_call(
        _combine_kernel,
        grid_spec=pltpu.PrefetchScalarGridSpec(
            num_scalar_prefetch=2,
            grid=(t // tc,),
            in_specs=[pl.BlockSpec(memory_space=pl.ANY),
                      pl.BlockSpec((tc, D_MODEL), lambda i, p0, p1: (i, 0)),
                      pl.BlockSpec((1, 6, D_MODEL), lambda i, p0, p1: (mrow(i), 0, 0)),
                      pl.BlockSpec((1, D_MODEL), lambda i, p0, p1: (0, 0))],
            out_specs=pl.BlockSpec((tc, D_MODEL), lambda i, p0, p1: (i, 0)),
            scratch_shapes=[pltpu.VMEM((tc, D_MODEL), F32), pltpu.VMEM((tc, D_MODEL), F32),
                            pltpu.SemaphoreType.DMA(()), pltpu.SemaphoreType.DMA(())]),
        out_shape=jax.ShapeDtypeStruct((t, D_MODEL), F32),
        compiler_params=_params("arbitrary"),
        name="moe_combine",
    )(p0, p1, y, x, mod, g_post)


def _reorder_w_in(w):
    gate0 = _OFF_BO + B_W
    body = jnp.concatenate([w[:, :gate0], w[:, gate0 + N_GATES:]], axis=1)
    gates = w[:, gate0:gate0 + N_GATES]
    pad = jnp.zeros((w.shape[0], LANES - N_GATES), w.dtype)
    return jnp.concatenate([body, gates, pad], axis=1).astype(BF16)


def _rope_tables(n):
    tok = jnp.arange(n, dtype=jnp.int32)
    pos = jnp.stack([tok // GRID_W, tok % GRID_W], axis=-1).astype(F32)
    inv = ROPE_THETA ** (-jnp.arange(ROPE_QUARTER, dtype=F32) / ROPE_QUARTER)
    ang = pos[:, :, None] * inv
    cos, sin = jnp.cos(ang), jnp.sin(ang)
    cos_t = jnp.concatenate([cos[:, 0], cos[:, 0], cos[:, 1], cos[:, 1]], axis=-1)
    sin_t = jnp.concatenate([-sin[:, 0], sin[:, 0], -sin[:, 1], sin[:, 1]], axis=-1)
    return cos_t, sin_t


def _head_major(cache):
    b, l, h, d = cache.shape
    return jnp.transpose(cache, (2, 0, 1, 3)).reshape(h, b * l, d).astype(BF16)


def kernel(x_prompt, x_sample, cache_gqa_k, cache_gqa_v, cache_na_k, cache_na_v, state_mlstm_C, state_mlstm_n, state_mlstm_m, c, c_ctx, w_mod, b_mod, g_pre_mix, g_post_mix, g_pre_ffn, g_post_ffn, w_in, w_out, g_q, g_k, mlstm_gate_bias, g_mlstm_out, na_rpb, w_ffn_gate, w_ffn_up, w_ffn_down, w_router, w_exp_gate, w_exp_up, w_exp_down):
    batch, seq, _ = x_prompt.shape
    dec_batch, dec_seq, _ = x_sample.shape
    assert seq % TOKEN_TILE == 0 or TOKEN_TILE % seq == 0
    assert dec_seq % (2 * TOKEN_TILE) == 0 and dec_seq % (NA_Q_ROWS * GRID_W) == 0
    tp, ts = batch * seq, dec_batch * dec_seq
    xp = x_prompt.reshape(tp, D_MODEL)
    xs = x_sample.reshape(ts, D_MODEL)

    n_mod = 8
    cvecs = jnp.concatenate([c_ctx[None, :], c, jnp.zeros((n_mod - 1 - dec_batch, D_MODEL), F32)], axis=0)
    mod_all = _modulation(cvecs, w_mod, b_mod)
    prompt_row = lambda i: 0
    tiles_per_sample = dec_seq // TOKEN_TILE
    sample_row = lambda i: 1 + i // tiles_per_sample

    rope_tabs = _rope_tables(dec_seq)
    row2 = lambda v: v.reshape(1, -1)
    ns = 2 * B_HEADS
    zeros_c = jnp.zeros((batch, ns, HEAD_DIM, LANES), F32)
    zeros_m = jnp.zeros((batch, ns, LANES), F32)

    ka_l, va_l, kc_l, vc_l, cst_l, m_l = [], [], [], [], [], []
    for l in range(DEPTH):
        mod = mod_all[l]
        w_in_l = _reorder_w_in(w_in[l])
        w_out_l = w_out[l].astype(BF16)
        gbias = jnp.concatenate([mlstm_gate_bias[l].reshape(1, N_GATES),
                                 jnp.zeros((1, LANES - N_GATES), F32)], axis=1)
        gout = g_mlstm_out[l].reshape(B_HEADS, 1, HEAD_DIM)
        moe = l % 2 == 1
        if moe:
            w_r = jnp.concatenate([w_router[l // 2], jnp.zeros((D_MODEL, LANES - N_EXPERTS), F32)], axis=1)
            ffn_w = tuple(w[l // 2].astype(BF16) for w in (w_exp_gate, w_exp_up, w_exp_down))
        else:
            w_r = None
            ffn_w = tuple(w[l // 2].astype(BF16) for w in (w_ffn_gate, w_ffn_up, w_ffn_down))

        def channel_mixer(res, x_mid, mod_row):
            if moe:
                return _moe_ffn(res[1], res[2], *ffn_w, x_mid, mod, mod_row, row2(g_post_ffn[l]))
            return _dense_ffn(res[1], *ffn_w, x_mid, mod, mod_row, row2(g_post_ffn[l]))

        (qa, ka, va, qb, kb, vb, ob, gb, qc, kc, vc, cka, cva, ckc, cvc) = _in_projection(
            xp, mod, prompt_row, row2(g_pre_mix[l]), w_in_l, row2(g_q[l]), row2(g_k[l]), None, True)
        a_out = _attention(qa, ka, va, seq, seq)
        b_out, cst, mst = _mlstm(qb, kb, vb, ob, gb, gbias, gout, zeros_c, zeros_m, seq)
        c_out = _attention(qc, kc, vc, seq, seq)
        res = _out_projection(a_out, b_out, c_out, w_out_l, xp, mod, prompt_row,
                              row2(g_post_mix[l]), row2(g_pre_ffn[l]), w_r)
        xp = channel_mixer(res, res[0], prompt_row)
        ka_l.append(cka.reshape(batch, seq, A_KV_HEADS, HEAD_DIM))
        va_l.append(cva.reshape(batch, seq, A_KV_HEADS, HEAD_DIM))
        kc_l.append(ckc.reshape(batch, seq, C_HEADS, HEAD_DIM))
        vc_l.append(cvc.reshape(batch, seq, C_HEADS, HEAD_DIM))
        cst_l.append(cst.reshape(batch, 2, B_HEADS, HEAD_DIM, LANES))
        m_l.append(mst.reshape(batch, 2, B_HEADS, LANES)[..., 0])

        (qa, ka, va, qb, kb, vb, ob, gb, qc, kc, vc) = _in_projection(
            xs, mod, sample_row, row2(g_pre_mix[l]), w_in_l, row2(g_q[l]), row2(g_k[l]), rope_tabs, False)
        a_out = _attention(qa, ka, va, dec_seq, 256,
                           ctx=(_head_major(cache_gqa_k[:, l]), _head_major(cache_gqa_v[:, l])))
        c0 = jnp.concatenate([state_mlstm_C[:, l], state_mlstm_n[:, l][..., None],
                              jnp.zeros((dec_batch, 2, B_HEADS, HEAD_DIM, LANES - HEAD_DIM - 1), F32)],
                             axis=-1).reshape(dec_batch, ns, HEAD_DIM, LANES)
        m0 = jnp.broadcast_to(state_mlstm_m[:, l].reshape(dec_batch, ns, 1), (dec_batch, ns, LANES))
        b_out, _, _ = _mlstm(qb, kb, vb, ob, gb, gbias, gout, c0, m0, dec_seq)
        c_out = _neighbourhood_attention(qc, kc, vc, _head_major(cache_na_k[:, l]),
                                         _head_major(cache_na_v[:, l]),
                                         _na_toeplitz(na_rpb[l]), dec_seq)
        res = _out_projection(a_out, b_out, c_out, w_out_l, xs, mod, sample_row,
                              row2(g_post_mix[l]), row2(g_pre_ffn[l]), w_r)
        xs = channel_mixer(res, res[0], sample_row)

    cst = jnp.stack(cst_l, axis=1)
    return (xp.reshape(batch, seq, D_MODEL), xs.reshape(dec_batch, dec_seq, D_MODEL),
            jnp.stack(ka_l, axis=1), jnp.stack(va_l, axis=1),
            jnp.stack(kc_l, axis=1), jnp.stack(vc_l, axis=1),
            cst[..., :HEAD_DIM], cst[..., HEAD_DIM], jnp.stack(m_l, axis=1))
```

```python
import functools

import jax
import jax.numpy as jnp
from jax import lax
from jax.experimental import pallas as pl
from jax.experimental.pallas import tpu as pltpu

D_MODEL = 1024
DEPTH = 2
GRID_W = 64
HEAD_DIM = 64
A_HEADS = 6
A_KV_HEADS = 2
A_GROUP = A_HEADS // A_KV_HEADS
B_HEADS = 4
C_HEADS = 6
A_Q_W = A_HEADS * HEAD_DIM
A_KV_W = A_KV_HEADS * HEAD_DIM
B_W = B_HEADS * HEAD_DIM
C_W = C_HEADS * HEAD_DIM
MLSTM_CHUNK = 64
NA_WIN_R = 8
NA_WIN_C = 16
ROPE_THETA = 10000.0
ROPE_QUARTER = HEAD_DIM // 4
ATTN_SCALE = HEAD_DIM ** -0.5
N_EXPERTS = 8
EPS = 1e-6

LANES = 128
N_GATES = 4 * B_HEADS
MASKED = -1e30

_OFF_AQ = 0
_OFF_AK = _OFF_AQ + A_Q_W
_OFF_AV = _OFF_AK + A_KV_W
_OFF_BQ = _OFF_AV + A_KV_W
_OFF_BK = _OFF_BQ + B_W
_OFF_BV = _OFF_BK + B_W
_OFF_BO = _OFF_BV + B_W
_OFF_CQ = _OFF_BO + B_W
_OFF_CK = _OFF_CQ + C_W
_OFF_CV = _OFF_CK + C_W
_OFF_BG = _OFF_CV + C_W
P_IN_PAD = _OFF_BG + LANES

TOKEN_TILE = 512
NA_Q_ROWS = 8
NA_K_ROWS = 16
ATTN_STEP_KEYS = 2048
ATTN_ROW_PARTS = 2
MLSTM_STEP_TOKENS = 1024
VMEM_LIMIT = 56 * 1024 * 1024

F32 = jnp.float32
BF16 = jnp.bfloat16
HIGHEST = lax.Precision.HIGHEST
_NT = (((1,), (1,)), ((), ()))


def _params(*sem):
    return pltpu.CompilerParams(dimension_semantics=sem, vmem_limit_bytes=VMEM_LIMIT)


def _rms(x, g):
    return x * lax.rsqrt(jnp.mean(x * x, axis=-1, keepdims=True) + EPS) * g


def _silu(x):
    return x * jax.nn.sigmoid(x)


def _mod_kernel(c_ref, w_ref, b_ref, o_ref):
    s = _silu(c_ref[...])
    o_ref[0] = jnp.dot(s, w_ref[0], precision=HIGHEST, preferred_element_type=F32) + b_ref[0]


def _modulation(cvecs, w_mod, b_mod):
    nb = cvecs.shape[0]
    tn = 1536
    out = pl.pallas_call(
        _mod_kernel,
        grid=(DEPTH, 6 * D_MODEL // tn),
        in_specs=[pl.BlockSpec((nb, D_MODEL), lambda l, j: (0, 0)),
                  pl.BlockSpec((1, D_MODEL, tn), lambda l, j: (l, 0, j)),
                  pl.BlockSpec((1, 1, tn), lambda l, j: (l, 0, j))],
        out_specs=pl.BlockSpec((1, nb, tn), lambda l, j: (l, 0, j)),
        out_shape=jax.ShapeDtypeStruct((DEPTH, nb, 6 * D_MODEL), F32),
        compiler_params=_params("parallel", "parallel"),
        name="adaln_mod",
    )(cvecs, w_mod, b_mod.reshape(DEPTH, 1, 6 * D_MODEL))
    return out.reshape(DEPTH, nb, 6, D_MODEL)


def _inproj_kernel(*refs, rope, caches):
    x_ref, mod_ref, g_ref, w_ref, gq_ref, gk_ref = refs[:6]
    pos = 6
    if rope:
        cos_ref, sin_ref = refs[pos:pos + 2]
        pos += 2
    (qa_ref, ka_ref, va_ref, qb_ref, kb_ref, vb_ref, ob_ref, gb_ref,
     qc_ref, kc_ref, vc_ref) = refs[pos:pos + 11]
    pos += 11
    if caches:
        cka_ref, cva_ref, ckc_ref, cvc_ref = refs[pos:pos + 4]

    m = mod_ref[0]
    h = _rms(x_ref[...], g_ref[...]) * (1.0 + m[1:2]) + m[0:1]
    y = jnp.dot(h.astype(BF16), w_ref[...], preferred_element_type=F32)

    def head(off, j):
        return y[:, off + j * HEAD_DIM: off + (j + 1) * HEAD_DIM]

    def rotary(p):
        if not rope:
            return p
        swapped = jnp.concatenate([p[:, 16:32], p[:, 0:16], p[:, 48:64], p[:, 32:48]], axis=1)
        return p * cos_ref[...] + swapped * sin_ref[...]

    for j in range(A_HEADS):
        qa_ref[j] = (rotary(_rms(head(_OFF_AQ, j), gq_ref[...])) * ATTN_SCALE).astype(BF16)
    for j in range(A_KV_HEADS):
        kn = _rms(head(_OFF_AK, j), gk_ref[...])
        ka_ref[j] = rotary(kn).astype(BF16)
        va_ref[j] = head(_OFF_AV, j).astype(BF16)
        if caches:
            cka_ref[:, j * HEAD_DIM:(j + 1) * HEAD_DIM] = kn
    lane = lax.broadcasted_iota(jnp.int32, (y.shape[0], LANES), 1)
    ones_col = jnp.where(lane == HEAD_DIM, 1.0, 0.0)
    for j in range(B_HEADS):
        qb_ref[j] = (head(_OFF_BQ, j) * ATTN_SCALE).astype(BF16)
        kb_ref[j] = head(_OFF_BK, j).astype(BF16)
        wide = y[:, _OFF_BV + j * HEAD_DIM: _OFF_BV + j * HEAD_DIM + LANES]
        vb_ref[j] = jnp.where(lane < HEAD_DIM, wide, ones_col).astype(BF16)
    ob_ref[...] = y[:, _OFF_BO:_OFF_BO + B_W]
    gb_ref[...] = y[:, _OFF_BG:_OFF_BG + LANES]
    for j in range(C_HEADS):
        qc_ref[j] = (head(_OFF_CQ, j) * ATTN_SCALE).astype(BF16)
        kc_ref[j] = head(_OFF_CK, j).astype(BF16)
        vc_ref[j] = head(_OFF_CV, j).astype(BF16)
    if caches:
        cva_ref[...] = y[:, _OFF_AV:_OFF_AV + A_KV_W]
        ckc_ref[...] = y[:, _OFF_CK:_OFF_CK + C_W]
        cvc_ref[...] = y[:, _OFF_CV:_OFF_CV + C_W]


def _in_projection(x, mod, mod_row, g_pre, w_in, g_q, g_k, rope_tabs, caches):
    t = x.shape[0]
    tm = TOKEN_TILE
    rope = rope_tabs is not None
    row = lambda i: (i, 0)
    heads = lambda i: (0, i, 0)
    fixed2 = lambda i: (0, 0)
    in_specs = [pl.BlockSpec((tm, D_MODEL), row),
                pl.BlockSpec((1, 6, D_MODEL), lambda i: (mod_row(i), 0, 0)),
                pl.BlockSpec((1, D_MODEL), fixed2),
                pl.BlockSpec((D_MODEL, P_IN_PAD), fixed2),
                pl.BlockSpec((1, HEAD_DIM), fixed2),
                pl.BlockSpec((1, HEAD_DIM), fixed2)]
    args = [x, mod, g_pre, w_in, g_q, g_k]
    if rope:
        n_pos = rope_tabs[0].shape[0] // tm
        in_specs += [pl.BlockSpec((tm, HEAD_DIM), lambda i: (i % n_pos, 0))] * 2
        args += list(rope_tabs)

    def hm(nh, width=HEAD_DIM):
        return (jax.ShapeDtypeStruct((nh, t, width), BF16), pl.BlockSpec((nh, tm, width), heads))

    def tokmajor(width):
        return (jax.ShapeDtypeStruct((t, width), F32), pl.BlockSpec((tm, width), row))

    outs = [hm(A_HEADS), hm(A_KV_HEADS), hm(A_KV_HEADS),
            hm(B_HEADS), hm(B_HEADS), hm(B_HEADS, LANES), tokmajor(B_W), tokmajor(LANES),
            hm(C_HEADS), hm(C_HEADS), hm(C_HEADS)]
    if caches:
        outs += [tokmajor(A_KV_W), tokmajor(A_KV_W), tokmajor(C_W), tokmajor(C_W)]
    return pl.pallas_call(
        functools.partial(_inproj_kernel, rope=rope, caches=caches),
        grid=(t // tm,),
        in_specs=in_specs,
        out_specs=[o[1] for o in outs],
        out_shape=[o[0] for o in outs],
        compiler_params=_params("parallel"),
        name="in_projection",
    )(*args)


def _softmax_attend(q, keys, values, bias_ref):
    step = q.shape[0] // ATTN_ROW_PARTS
    scores = []
    for part in range(ATTN_ROW_PARTS):
        qp = q[part * step:(part + 1) * step]
        sc = [lax.dot_general(qp, kb, _NT, preferred_element_type=F32) for kb in keys]
        if bias_ref is not None:
            sc[0] = sc[0] + bias_ref[part * step:(part + 1) * step, :]
        scores.append(sc)
    probs = []
    for sc in scores:
        mx = functools.reduce(jnp.maximum, [jnp.max(s, axis=-1, keepdims=True) for s in sc])
        ps = [jnp.exp(s - mx) for s in sc]
        den = functools.reduce(lambda a, b: a + b, [jnp.sum(p, axis=-1, keepdims=True) for p in ps])
        probs.append((ps, den))
    outs = []
    for ps, den in probs:
        o = functools.reduce(lambda a, b: a + b,
                             [jnp.dot(p.astype(BF16), vb, preferred_element_type=F32)
                              for p, vb in zip(ps, values)])
        outs.append(o / den)
    return jnp.concatenate(outs, axis=0)


def _attn_kernel(*refs, ctx):
    if ctx:
        q_ref, k_ref, v_ref, kx_ref, vx_ref, o_ref = refs
    else:
        q_ref, k_ref, v_ref, o_ref = refs
    hb = k_ref.shape[0]
    g, tq, d = q_ref.shape[0] // hb, q_ref.shape[1], q_ref.shape[2]
    for kv in range(hb):
        q = q_ref[kv * g:(kv + 1) * g].reshape(g * tq, d)
        keys, values = [k_ref[kv]], [v_ref[kv]]
        if ctx:
            keys.append(kx_ref[kv])
            values.append(vx_ref[kv])
        o_ref[kv * g:(kv + 1) * g] = _softmax_attend(q, keys, values, None).reshape(g, tq, d)


def _attention(q, k, v, n_seq, tq, ctx=None):
    hq, t, d = q.shape
    hkv = k.shape[0]
    g = hq // hkv
    nb = t // n_seq
    nq = n_seq // tq
    hb = hkv if hkv * n_seq <= ATTN_STEP_KEYS else 1
    in_specs = [pl.BlockSpec((hb * g, tq, d), lambda b, kv, i: (kv, b * nq + i, 0)),
                pl.BlockSpec((hb, n_seq, d), lambda b, kv, i: (kv, b, 0)),
                pl.BlockSpec((hb, n_seq, d), lambda b, kv, i: (kv, b, 0))]
    args = [q, k, v]
    if ctx is not None:
        lc = ctx[0].shape[1] // nb
        in_specs += [pl.BlockSpec((hb, lc, d), lambda b, kv, i: (kv, b, 0))] * 2
        args += list(ctx)
    return pl.pallas_call(
        functools.partial(_attn_kernel, ctx=ctx is not None),
        grid=(nb, hkv // hb, nq),
        in_specs=in_specs,
        out_specs=pl.BlockSpec((hb * g, tq, d), lambda b, kv, i: (kv, b * nq + i, 0)),
        out_shape=jax.ShapeDtypeStruct((hq, t, d), F32),
        compiler_params=_params("parallel", "parallel", "parallel"),
        name="dense_attention",
    )(*args)


def _na_kernel(q_ref, k_ref, v_ref, kx_ref, vx_ref, toep_ref, o_ref, bias_scr):
    i = pl.program_id(1)
    nk = bias_scr.shape[1]
    rows = k_ref.shape[1] // GRID_W
    first_row = jnp.clip(NA_Q_ROWS * i - NA_WIN_R // 2, 0, rows - NA_K_ROWS)

    @pl.when(pl.program_id(2) == 0)
    def _():
        for a in range(NA_Q_ROWS):
            qr = NA_Q_ROWS * i + a
            r_start = jnp.clip(qr - NA_WIN_R // 2, 0, rows - NA_WIN_R)
            for w in range(NA_K_ROWS):
                kr = first_row + w
                in_window = (kr >= r_start) & (kr < r_start + NA_WIN_R)
                dr = jnp.clip(kr - qr + NA_WIN_R - 1, 0, 2 * NA_WIN_R - 2)
                bias_scr[a * GRID_W:(a + 1) * GRID_W, w * GRID_W:(w + 1) * GRID_W] = (
                    toep_ref[0, dr] + jnp.where(in_window, 0.0, MASKED))

    start = pl.multiple_of(first_row * GRID_W, GRID_W)
    q = q_ref[0]
    k = k_ref[0, pl.ds(start, nk), :]
    v = v_ref[0, pl.ds(start, nk), :]
    o_ref[0] = _softmax_attend(q, [k, kx_ref[0]], [v, vx_ref[0]], bias_scr)


def _na_toeplitz(rpb):
    qc = jnp.arange(GRID_W, dtype=jnp.int32)[:, None]
    kc = jnp.arange(GRID_W, dtype=jnp.int32)[None, :]
    c_start = jnp.clip(qc - NA_WIN_C // 2, 0, GRID_W - NA_WIN_C)
    in_window = (kc >= c_start) & (kc < c_start + NA_WIN_C)
    offs = jnp.arange(2 * NA_WIN_C - 1, dtype=jnp.int32)[:, None, None]
    onehot = ((kc - qc + NA_WIN_C - 1)[None] == offs) & in_window[None]
    table = jnp.einsum('hrb,bqk->hrqk', rpb.astype(F32), onehot.astype(F32), precision=HIGHEST)
    return table + jnp.where(in_window, 0.0, MASKED)


def _neighbourhood_attention(q, k, v, kx, vx, toep, n_seq):
    h, t, d = q.shape
    nb = t // n_seq
    tq = NA_Q_ROWS * GRID_W
    nblk = n_seq // tq
    lc = kx.shape[1] // nb
    return pl.pallas_call(
        _na_kernel,
        grid=(h, nblk, nb),
        in_specs=[pl.BlockSpec((1, tq, d), lambda hh, i, b: (hh, b * nblk + i, 0)),
                  pl.BlockSpec((1, n_seq, d), lambda hh, i, b: (hh, b, 0)),
                  pl.BlockSpec((1, n_seq, d), lambda hh, i, b: (hh, b, 0)),
                  pl.BlockSpec((1, lc, d), lambda hh, i, b: (hh, b, 0)),
                  pl.BlockSpec((1, lc, d), lambda hh, i, b: (hh, b, 0)),
                  pl.BlockSpec((1,) + toep.shape[1:], lambda hh, i, b: (hh, 0, 0, 0))],
        out_specs=pl.BlockSpec((1, tq, d), lambda hh, i, b: (hh, b * nblk + i, 0)),
        out_shape=jax.ShapeDtypeStruct((h, t, d), F32),
        scratch_shapes=[pltpu.VMEM((tq, NA_K_ROWS * GRID_W), F32)],
        compiler_params=_params("parallel", "parallel", "arbitrary"),
        name="neighbourhood_attention",
    )(q, k, v, kx, vx, toep)


def _mlstm_kernel(q_ref, k_ref, v_ref, o_ref, g_ref, gbias_ref, gout_ref, c0_ref, m0_ref,
                  h_ref, cn_ref, mn_ref, hs_fwd, hs_bwd):
    L = MLSTM_CHUNK
    nseq, ns = m0_ref.shape[:2]
    n = q_ref.shape[1] // nseq
    nc = n // L
    hs_dir = (hs_fwd, hs_bwd)

    row = lax.broadcasted_iota(jnp.int32, (L, L), 0)
    col = lax.broadcasted_iota(jnp.int32, (L, L), 1)
    masks = (col <= row, col >= row)
    cum_mats = tuple(mk.astype(F32) for mk in masks)
    lane = lax.broadcasted_iota(jnp.int32, (L, LANES), 1)
    eye_l = (lax.broadcasted_iota(jnp.int32, (LANES, LANES), 0)
             == lax.broadcasted_iota(jnp.int32, (LANES, LANES), 1)).astype(F32)
    eye_k = (row == col).astype(BF16)

    def column(xmat, j):
        return jnp.sum(jnp.where(lane == j, xmat, 0.0), axis=1, keepdims=True)

    def body(i, carry):
        cs, ms = list(carry[0]), list(carry[1])
        groups = [(sq, d) for sq in range(nseq) for d in range(2)]
        streams = [(sq, d, hh) for sq, d in groups for hh in range(B_HEADS)]
        rows_of, pre_of, cum_of, pre_t_of, cum_t_of = {}, {}, {}, {}, {}
        for sq, d in groups:
            c = i if d == 0 else nc - 1 - i
            rows_of[sq, d] = pl.ds(pl.multiple_of(sq * n + c * L, L), L)
            pre = g_ref[rows_of[sq, d], :] + gbias_ref[...]
            logf = jnp.minimum(pre, 0.0) - jnp.log1p(jnp.exp(-jnp.abs(pre)))
            pre_of[sq, d] = pre
            cum_of[sq, d] = jnp.dot(cum_mats[d], logf, precision=HIGHEST, preferred_element_type=F32)
        for key in groups:
            pre_t_of[key] = lax.dot_general(eye_l, pre_of[key], _NT, precision=HIGHEST,
                                            preferred_element_type=F32)
            cum_t_of[key] = lax.dot_general(eye_l, cum_of[key], _NT, precision=HIGHEST,
                                            preferred_element_type=F32)

        idx_of = {st: st[0] * ns + st[1] * B_HEADS + st[2] for st in streams}
        qkv, qk_of, qc_of, kt_of = {}, {}, {}, {}
        for st in streams:
            sq, d, hh = st
            rows = rows_of[sq, d]
            qc, kc, vc = q_ref[hh, rows, :], k_ref[hh, rows, :], v_ref[hh, rows, :]
            qkv[st] = (qc, kc, vc)
            qk_of[st] = lax.dot_general(qc, kc, _NT, preferred_element_type=F32)
            kt_of[st] = lax.dot_general(eye_k, kc, _NT, preferred_element_type=F32).astype(BF16)
        for st in streams:
            qc_of[st] = jnp.dot(qkv[st][0], cs[idx_of[st]].astype(BF16), preferred_element_type=F32)

        gate_of = {}
        for st in streams:
            sq, d, hh = st
            ji = d * 2 * B_HEADS + hh
            jf = ji + B_HEADS
            b_col = column(cum_of[sq, d], jf)
            li_col = column(pre_of[sq, d], ji)
            b_row = cum_t_of[sq, d][jf:jf + 1, :]
            li_row = pre_t_of[sq, d][ji:ji + 1, :]
            m_prev = ms[idx_of[st]][:, 0:1]
            log_d = jnp.where(masks[d], b_col - b_row + li_row, -jnp.inf)
            m_inter = b_col + m_prev
            m_t = jnp.maximum(m_inter, jnp.max(log_d, axis=-1, keepdims=True))
            gate_of[st] = (b_col, li_col, m_prev, m_t, jnp.exp(m_inter - m_t), jnp.exp(log_d - m_t))

        num_of = {}
        for st in streams:
            w_inter, decay_mat = gate_of[st][4], gate_of[st][5]
            s = qk_of[st] * decay_mat
            num_of[st] = (jnp.dot(s.astype(BF16), qkv[st][2], preferred_element_type=F32)
                          + w_inter * qc_of[st])

        for st in streams:
            sq, d, hh = st
            b_col, li_col, m_prev, m_t = gate_of[st][:4]
            num = num_of[st]
            den = column(num, HEAD_DIM)
            hs_dir[d][hh, rows_of[sq, d], :] = (num[:, :HEAD_DIM]
                                               / jnp.maximum(jnp.abs(den), jnp.exp(-m_t)))
            last = L - 1 if d == 0 else 0
            m_new = m_t[last:last + 1, :]
            b_last = b_col[last:last + 1, :]
            w = jnp.exp(b_last - b_col + li_col - m_new)
            decay = jnp.exp(b_last + m_prev - m_new)
            wv = (w * qkv[st][2].astype(F32)).astype(BF16)
            cs[idx_of[st]] = decay * cs[idx_of[st]] + jnp.dot(kt_of[st], wv, preferred_element_type=F32)
            ms[idx_of[st]] = jnp.broadcast_to(m_new, (1, LANES))
        return tuple(cs), tuple(ms)

    streams = [(sq, s) for sq in range(nseq) for s in range(ns)]
    cs, ms = lax.fori_loop(0, nc, body, (tuple(c0_ref[sq, s] for sq, s in streams),
                                         tuple(m0_ref[sq, s:s + 1, :] for sq, s in streams)))

    for hh in range(B_HEADS):
        gate = jax.nn.sigmoid(o_ref[:, hh * HEAD_DIM:(hh + 1) * HEAD_DIM])
        h_ref[hh] = _rms(hs_fwd[hh] + hs_bwd[hh], gout_ref[hh]) * gate
    for idx, (sq, s) in enumerate(streams):
        cn_ref[sq, s] = cs[idx]
        mn_ref[sq, s:s + 1, :] = ms[idx]


def _mlstm(q, k, v, o, g, gate_bias, g_out, c0, m0, n_seq):
    h, t, d = q.shape
    nsq = max(1, MLSTM_STEP_TOKENS // n_seq)
    nb = t // n_seq
    assert nb % nsq == 0
    ns = 2 * h
    heads = lambda b: (0, b, 0)
    return pl.pallas_call(
        _mlstm_kernel,
        grid=(nb // nsq,),
        in_specs=[pl.BlockSpec((h, nsq * n_seq, d), heads),
                  pl.BlockSpec((h, nsq * n_seq, d), heads),
                  pl.BlockSpec((h, nsq * n_seq, LANES), heads),
                  pl.BlockSpec((nsq * n_seq, h * d), lambda b: (b, 0)),
                  pl.BlockSpec((nsq * n_seq, LANES), lambda b: (b, 0)),
                  pl.BlockSpec((1, LANES), lambda b: (0, 0)),
                  pl.BlockSpec((h, 1, d), lambda b: (0, 0, 0)),
                  pl.BlockSpec((nsq, ns, d, LANES), lambda b: (b, 0, 0, 0)),
                  pl.BlockSpec((nsq, ns, LANES), lambda b: (b, 0, 0))],
        out_specs=[pl.BlockSpec((h, nsq * n_seq, d), heads),
                   pl.BlockSpec((nsq, ns, d, LANES), lambda b: (b, 0, 0, 0)),
                   pl.BlockSpec((nsq, ns, LANES), lambda b: (b, 0, 0))],
        out_shape=[jax.ShapeDtypeStruct((h, t, d), F32),
                   jax.ShapeDtypeStruct((nb, ns, d, LANES), F32),
                   jax.ShapeDtypeStruct((nb, ns, LANES), F32)],
        scratch_shapes=[pltpu.VMEM((h, nsq * n_seq, d), F32), pltpu.VMEM((h, nsq * n_seq, d), F32)],
        compiler_params=_params("parallel"),
        name="mlstm",
    )(q, k, v, o, g, gate_bias, g_out, c0, m0)


def _outproj_kernel(*refs, moe):
    a_ref, b_ref, c_ref, w_ref, x_ref, mod_ref, gpost_ref, gpre_ref = refs[:8]
    pos = 8
    if moe:
        wr_ref = refs[pos]
        pos += 1
    xo_ref, h2_ref = refs[pos:pos + 2]
    pos += 2
    if moe:
        gates_ref = refs[pos]
        pos += 1
    cat_ref = refs[pos]

    off = 0
    for src in (a_ref, b_ref, c_ref):
        for j in range(src.shape[0]):
            cat_ref[:, off:off + HEAD_DIM] = src[j]
            off += HEAD_DIM
    out = jnp.dot(cat_ref[...].astype(BF16), w_ref[...], preferred_element_type=F32)
    m = mod_ref[0]
    xn = x_ref[...] + m[2:3] * _rms(out, gpost_ref[...])
    xo_ref[...] = xn
    h2 = _rms(xn, gpre_ref[...]) * (1.0 + m[4:5]) + m[3:4]
    h2_ref[...] = h2.astype(h2_ref.dtype)
    if moe:
        logits = jnp.dot(h2, wr_ref[...], precision=HIGHEST, preferred_element_type=F32)
        lane = lax.broadcasted_iota(jnp.int32, logits.shape, 1)
        logits = jnp.where(lane < N_EXPERTS, logits, -jnp.inf)
        v1 = jnp.max(logits, axis=-1, keepdims=True)
        i1 = jnp.min(jnp.where(logits == v1, lane, LANES), axis=-1, keepdims=True)
        rest = jnp.where(lane == i1, -jnp.inf, logits)
        v2 = jnp.max(rest, axis=-1, keepdims=True)
        i2 = jnp.min(jnp.where(rest == v2, lane, LANES), axis=-1, keepdims=True)
        e2 = jnp.exp(v2 - v1)
        w1 = 1.0 / (1.0 + e2)
        w2 = e2 / (1.0 + e2)
        gates_ref[...] = jnp.where(lane == i1, w1, 0.0) + jnp.where(lane == i2, w2, 0.0)


def _out_projection(a, b, c, w_out, x, mod, mod_row, g_post, g_pre, w_router):
    t = x.shape[0]
    tm = TOKEN_TILE
    moe = w_router is not None
    row = lambda i: (i, 0)
    fixed2 = lambda i: (0, 0)
    heads = lambda i: (0, i, 0)
    in_specs = [pl.BlockSpec((a.shape[0], tm, HEAD_DIM), heads),
                pl.BlockSpec((b.shape[0], tm, HEAD_DIM), heads),
                pl.BlockSpec((c.shape[0], tm, HEAD_DIM), heads),
                pl.BlockSpec((D_MODEL, D_MODEL), fixed2),
                pl.BlockSpec((tm, D_MODEL), row),
                pl.BlockSpec((1, 6, D_MODEL), lambda i: (mod_row(i), 0, 0)),
                pl.BlockSpec((1, D_MODEL), fixed2),
                pl.BlockSpec((1, D_MODEL), fixed2)]
    args = [a, b, c, w_out, x, mod, g_post, g_pre]
    out_specs = [pl.BlockSpec((tm, D_MODEL), row), pl.BlockSpec((tm, D_MODEL), row)]
    out_shape = [jax.ShapeDtypeStruct((t, D_MODEL), F32),
                 jax.ShapeDtypeStruct((t, D_MODEL), F32 if moe else BF16)]
    if moe:
        in_specs.append(pl.BlockSpec((D_MODEL, LANES), fixed2))
        args.append(w_router)
        out_specs.append(pl.BlockSpec((tm, LANES), row))
        out_shape.append(jax.ShapeDtypeStruct((t, LANES), F32))
    return pl.pallas_call(
        functools.partial(_outproj_kernel, moe=moe),
        grid=(t // tm,),
        in_specs=in_specs,
        out_specs=out_specs,
        out_shape=out_shape,
        scratch_shapes=[pltpu.VMEM((tm, D_MODEL), F32)],
        compiler_params=_params("parallel"),
        name="out_projection",
    )(*args)


def _swiglu_chunk(h, wg, wu, wd):
    a = jnp.dot(h, wg, preferred_element_type=F32)
    b = jnp.dot(h, wu, preferred_element_type=F32)
    return jnp.dot((_silu(a) * b).astype(BF16), wd, preferred_element_type=F32)


def _ffn_kernel(h_ref, wg_ref, wu_ref, wd_ref, x_ref, mod_ref, g_ref, o_ref, acc_ref):
    j = pl.program_id(1)

    @pl.when(j == 0)
    def _():
        acc_ref[...] = jnp.zeros(acc_ref.shape, F32)

    acc_ref[...] += _swiglu_chunk(h_ref[...], wg_ref[...], wu_ref[...], wd_ref[...])

    @pl.when(j == pl.num_programs(1) - 1)
    def _():
        o_ref[...] = x_ref[...] + mod_ref[0][5:6] * _rms(acc_ref[...], g_ref[...])


def _dense_ffn(h2, wg, wu, wd, x, mod, mod_row, g_post):
    t = x.shape[0]
    tm = TOKEN_TILE
    ff = wg.shape[1]
    tf = ff // 2
    return pl.pallas_call(
        _ffn_kernel,
        grid=(t // tm, ff // tf),
        in_specs=[pl.BlockSpec((tm, D_MODEL), lambda i, j: (i, 0)),
                  pl.BlockSpec((D_MODEL, tf), lambda i, j: (0, j)),
                  pl.BlockSpec((D_MODEL, tf), lambda i, j: (0, j)),
                  pl.BlockSpec((tf, D_MODEL), lambda i, j: (j, 0)),
                  pl.BlockSpec((tm, D_MODEL), lambda i, j: (i, 0)),
                  pl.BlockSpec((1, 6, D_MODEL), lambda i, j: (mod_row(i), 0, 0)),
                  pl.BlockSpec((1, D_MODEL), lambda i, j: (0, 0))],
        out_specs=pl.BlockSpec((tm, D_MODEL), lambda i, j: (i, 0)),
        out_shape=jax.ShapeDtypeStruct((t, D_MODEL), F32),
        scratch_shapes=[pltpu.VMEM((tm, D_MODEL), F32)],
        compiler_params=_params("parallel", "arbitrary"),
        name="dense_ffn",
    )(h2, wg, wu, wd, x, mod, g_post)


def _row_copy(src_hbm, src_row, dst, dst_row, sem):
    return pltpu.make_async_copy(src_hbm.at[pl.ds(src_row, 1)], dst.at[pl.ds(dst_row, 1)], sem)


def _gather_rows(src_hbm, idx_ref, base, dst, sem):
    n = dst.shape[0]

    def issue(r, carry):
        _row_copy(src_hbm, idx_ref[base + r], dst, r, sem).start()
        return carry

    def drain(r, carry):
        _row_copy(src_hbm, 0, dst, r, sem).wait()
        return carry

    lax.fori_loop(0, n, issue, 0, unroll=8)
    lax.fori_loop(0, n, drain, 0, unroll=8)


def _moe_kernel(te_ref, nv_ref, tok_ref, h_hbm, w_ref, wg_ref, wu_ref, wd_ref, y_ref,
                xbuf, xb16, acc_ref, sem):
    del te_ref
    i = pl.program_id(0)
    j = pl.program_id(1)
    used = i < nv_ref[0]

    @pl.when(used & (j == 0))
    def _():
        _gather_rows(h_hbm, tok_ref, i * xbuf.shape[0], xbuf, sem)
        xb16[...] = xbuf[...].astype(BF16)
        acc_ref[...] = jnp.zeros(acc_ref.shape, F32)

    @pl.when(used)
    def _():
        acc_ref[...] += _swiglu_chunk(xb16[...], wg_ref[0], wu_ref[0], wd_ref[0])

    last = j == pl.num_programs(1) - 1

    @pl.when(used & last)
    def _():
        y_ref[...] = w_ref[:, 0:1] * acc_ref[...]

    @pl.when(jnp.logical_not(used) & last)
    def _():
        y_ref[...] = jnp.zeros(y_ref.shape, F32)


def _combine_kernel(p0_ref, p1_ref, y_hbm, x_ref, mod_ref, g_ref, o_ref, buf0, buf1, sem0, sem1):
    base = pl.program_id(0) * buf0.shape[0]
    _gather_rows(y_hbm, p0_ref, base, buf0, sem0)
    _gather_rows(y_hbm, p1_ref, base, buf1, sem1)
    f = buf0[...] + buf1[...]
    o_ref[...] = x_ref[...] + mod_ref[0][5:6] * _rms(f, g_ref[...])


def _route(gates, tm, n_tiles):
    t = gates.shape[0]
    g = gates[:, :N_EXPERTS]
    mask = g != 0.0
    mi = mask.astype(jnp.int32)
    rank = jnp.cumsum(mi, axis=0) - mi
    tiles_e = (jnp.sum(mi, axis=0) + tm - 1) // tm
    tile_end = jnp.cumsum(tiles_e)
    n_used = tile_end[-1]
    zero_row = n_tiles * tm
    pos = (tile_end - tiles_e)[None, :] * tm + rank
    rows = zero_row + tm
    tile_expert = jnp.minimum(
        jnp.sum(jnp.arange(n_tiles + 1, dtype=jnp.int32)[:, None] >= tile_end[None, :], axis=1),
        N_EXPERTS - 1).astype(jnp.int32)
    order = jnp.cumsum(mi, axis=1)
    first, second = mask & (order == 1), mask & (order == 2)
    p0 = jnp.min(jnp.where(first, pos, zero_row), axis=1).astype(jnp.int32)
    p1 = jnp.min(jnp.where(second, pos, zero_row), axis=1).astype(jnp.int32)
    w0 = jnp.sum(jnp.where(first, g, 0.0), axis=1)
    w1 = jnp.sum(jnp.where(second, g, 0.0), axis=1)
    tok = jnp.arange(t, dtype=jnp.int32)
    dest = jnp.concatenate([p0, p1])
    tok_sorted = jnp.zeros((rows,), jnp.int32).at[dest].set(jnp.concatenate([tok, tok]))
    w_sorted = jnp.zeros((rows,), F32).at[dest].set(jnp.concatenate([w0, w1]))
    return (tok_sorted, jnp.broadcast_to(w_sorted[:, None], (rows, LANES)), tile_expert,
            n_used.reshape(1).astype(jnp.int32), p0, p1)


def _moe_ffn(h2, gates, wg, wu, wd, x, mod, mod_row, g_post):
    t = x.shape[0]
    tm = TOKEN_TILE
    ne, _, ff = wg.shape
    tf = 512
    nj = ff // tf
    n_tiles = 2 * t // tm + ne
    tok_sorted, w_sorted, tile_expert, n_used, p0, p1 = _route(gates, tm, n_tiles)

    def chunk(i, j, nv):
        return jnp.where(i < nv[0], j, nj - 1)

    y = pl.pallas_call(
        _moe_kernel,
        grid_spec=pltpu.PrefetchScalarGridSpec(
            num_scalar_prefetch=3,
            grid=(n_tiles + 1, nj),
            in_specs=[pl.BlockSpec(memory_space=pl.ANY),
                      pl.BlockSpec((tm, LANES), lambda i, j, te, nv, tok: (i, 0)),
                      pl.BlockSpec((1, D_MODEL, tf), lambda i, j, te, nv, tok: (te[i], 0, chunk(i, j, nv))),
                      pl.BlockSpec((1, D_MODEL, tf), lambda i, j, te, nv, tok: (te[i], 0, chunk(i, j, nv))),
                      pl.BlockSpec((1, tf, D_MODEL), lambda i, j, te, nv, tok: (te[i], chunk(i, j, nv), 0))],
            out_specs=pl.BlockSpec((tm, D_MODEL), lambda i, j, te, nv, tok: (i, 0)),
            scratch_shapes=[pltpu.VMEM((tm, D_MODEL), F32), pltpu.VMEM((tm, D_MODEL), BF16),
                            pltpu.VMEM((tm, D_MODEL), F32), pltpu.SemaphoreType.DMA(())]),
        out_shape=jax.ShapeDtypeStruct(((n_tiles + 1) * tm, D_MODEL), F32),
        compiler_params=_params("arbitrary", "arbitrary"),
        name="moe_experts",
    )(tile_expert, n_used, tok_sorted, h2, w_sorted, wg, wu, wd)

    tc = TOKEN_TILE // 2
    mrow = lambda i: mod_row(i * tc // TOKEN_TILE)
    return pl.pallas_call(
        _combine_kernel,
        grid_spec=pltpu.PrefetchScalarGridSpec(
            num_scalar_prefetch=2,
            grid=(t // tc,),
            in_specs=[pl.BlockSpec(memory_space=pl.ANY),
                      pl.BlockSpec((tc, D_MODEL), lambda i, p0, p1: (i, 0)),
                      pl.BlockSpec((1, 6, D_MODEL), lambda i, p0, p1: (mrow(i), 0, 0)),
                      pl.BlockSpec((1, D_MODEL), lambda i, p0, p1: (0, 0))],
            out_specs=pl.BlockSpec((tc, D_MODEL), lambda i, p0, p1: (i, 0)),
            scratch_shapes=[pltpu.VMEM((tc, D_MODEL), F32), pltpu.VMEM((tc, D_MODEL), F32),
                            pltpu.SemaphoreType.DMA(()), pltpu.SemaphoreType.DMA(())]),
        out_shape=jax.ShapeDtypeStruct((t, D_MODEL), F32),
        compiler_params=_params("arbitrary"),
        name="moe_combine",
    )(p0, p1, y, x, mod, g_post)


def _reorder_w_in(w):
    gate0 = _OFF_BO + B_W
    body = jnp.concatenate([w[:, :gate0], w[:, gate0 + N_GATES:]], axis=1)
    gates = w[:, gate0:gate0 + N_GATES]
    pad = jnp.zeros((w.shape[0], LANES - N_GATES), w.dtype)
    return jnp.concatenate([body, gates, pad], axis=1).astype(BF16)


def _rope_tables(n):
    tok = jnp.arange(n, dtype=jnp.int32)
    pos = jnp.stack([tok // GRID_W, tok % GRID_W], axis=-1).astype(F32)
    inv = ROPE_THETA ** (-jnp.arange(ROPE_QUARTER, dtype=F32) / ROPE_QUARTER)
    ang = pos[:, :, None] * inv
    cos, sin = jnp.cos(ang), jnp.sin(ang)
    cos_t = jnp.concatenate([cos[:, 0], cos[:, 0], cos[:, 1], cos[:, 1]], axis=-1)
    sin_t = jnp.concatenate([-sin[:, 0], sin[:, 0], -sin[:, 1], sin[:, 1]], axis=-1)
    return cos_t, sin_t


def _head_major(cache):
    b, l, h, d = cache.shape
    return jnp.transpose(cache, (2, 0, 1, 3)).reshape(h, b * l, d).astype(BF16)


def kernel(x_prompt, x_sample, cache_gqa_k, cache_gqa_v, cache_na_k, cache_na_v, state_mlstm_C, state_mlstm_n, state_mlstm_m, c, c_ctx, w_mod, b_mod, g_pre_mix, g_post_mix, g_pre_ffn, g_post_ffn, w_in, w_out, g_q, g_k, mlstm_gate_bias, g_mlstm_out, na_rpb, w_ffn_gate, w_ffn_up, w_ffn_down, w_router, w_exp_gate, w_exp_up, w_exp_down):
    batch, seq, _ = x_prompt.shape
    dec_batch, dec_seq, _ = x_sample.shape
    assert seq % TOKEN_TILE == 0 or TOKEN_TILE % seq == 0
    assert dec_seq % (2 * TOKEN_TILE) == 0 and dec_seq % (NA_Q_ROWS * GRID_W) == 0
    tp, ts = batch * seq, dec_batch * dec_seq
    xp = x_prompt.reshape(tp, D_MODEL)
    xs = x_sample.reshape(ts, D_MODEL)

    n_mod = 8
    cvecs = jnp.concatenate([c_ctx[None, :], c, jnp.zeros((n_mod - 1 - dec_batch, D_MODEL), F32)], axis=0)
    mod_all = _modulation(cvecs, w_mod, b_mod)
    prompt_row = lambda i: 0
    tiles_per_sample = dec_seq // TOKEN_TILE
    sample_row = lambda i: 1 + i // tiles_per_sample

    rope_tabs = _rope_tables(dec_seq)
    row2 = lambda v: v.reshape(1, -1)
    ns = 2 * B_HEADS
    zeros_c = jnp.zeros((batch, ns, HEAD_DIM, LANES), F32)
    zeros_m = jnp.zeros((batch, ns, LANES), F32)

    ka_l, va_l, kc_l, vc_l, cst_l, m_l = [], [], [], [], [], []
    for l in range(DEPTH):
        mod = mod_all[l]
        w_in_l = _reorder_w_in(w_in[l])
        w_out_l = w_out[l].astype(BF16)
        gbias = jnp.concatenate([mlstm_gate_bias[l].reshape(1, N_GATES),
                                 jnp.zeros((1, LANES - N_GATES), F32)], axis=1)
        gout = g_mlstm_out[l].reshape(B_HEADS, 1, HEAD_DIM)
        moe = l % 2 == 1
        if moe:
            w_r = jnp.concatenate([w_router[l // 2], jnp.zeros((D_MODEL, LANES - N_EXPERTS), F32)], axis=1)
            ffn_w = tuple(w[l // 2].astype(BF16) for w in (w_exp_gate, w_exp_up, w_exp_down))
        else:
            w_r = None
            ffn_w = tuple(w[l // 2].astype(BF16) for w in (w_ffn_gate, w_ffn_up, w_ffn_down))

        def channel_mixer(res, x_mid, mod_row):
            if moe:
                return _moe_ffn(res[1], res[2], *ffn_w, x_mid, mod, mod_row, row2(g_post_ffn[l]))
            return _dense_ffn(res[1], *ffn_w, x_mid, mod, mod_row, row2(g_post_ffn[l]))

        (qa, ka, va, qb, kb, vb, ob, gb, qc, kc, vc, cka, cva, ckc, cvc) = _in_projection(
            xp, mod, prompt_row, row2(g_pre_mix[l]), w_in_l, row2(g_q[l]), row2(g_k[l]), None, True)
        a_out = _attention(qa, ka, va, seq, seq)
        b_out, cst, mst = _mlstm(qb, kb, vb, ob, gb, gbias, gout, zeros_c, zeros_m, seq)
        c_out = _attention(qc, kc, vc, seq, seq)
        res = _out_projection(a_out, b_out, c_out, w_out_l, xp, mod, prompt_row,
                              row2(g_post_mix[l]), row2(g_pre_ffn[l]), w_r)
        xp = channel_mixer(res, res[0], prompt_row)
        ka_l.append(cka.reshape(batch, seq, A_KV_HEADS, HEAD_DIM))
        va_l.append(cva.reshape(batch, seq, A_KV_HEADS, HEAD_DIM))
        kc_l.append(ckc.reshape(batch, seq, C_HEADS, HEAD_DIM))
        vc_l.append(cvc.reshape(batch, seq, C_HEADS, HEAD_DIM))
        cst_l.append(cst.reshape(batch, 2, B_HEADS, HEAD_DIM, LANES))
        m_l.append(mst.reshape(batch, 2, B_HEADS, LANES)[..., 0])

        (qa, ka, va, qb, kb, vb, ob, gb, qc, kc, vc) = _in_projection(
            xs, mod, sample_row, row2(g_pre_mix[l]), w_in_l, row2(g_q[l]), row2(g_k[l]), rope_tabs, False)
        a_out = _attention(qa, ka, va, dec_seq, 256,
                           ctx=(_head_major(cache_gqa_k[:, l]), _head_major(cache_gqa_v[:, l])))
        c0 = jnp.concatenate([state_mlstm_C[:, l], state_mlstm_n[:, l][..., None],
                              jnp.zeros((dec_batch, 2, B_HEADS, HEAD_DIM, LANES - HEAD_DIM - 1), F32)],
                             axis=-1).reshape(dec_batch, ns, HEAD_DIM, LANES)
        m0 = jnp.broadcast_to(state_mlstm_m[:, l].reshape(dec_batch, ns, 1), (dec_batch, ns, LANES))
        b_out, _, _ = _mlstm(qb, kb, vb, ob, gb, gbias, gout, c0, m0, dec_seq)
        c_out = _neighbourhood_attention(qc, kc, vc, _head_major(cache_na_k[:, l]),
                                         _head_major(cache_na_v[:, l]),
                                         _na_toeplitz(na_rpb[l]), dec_seq)
        res = _out_projection(a_out, b_out, c_out, w_out_l, xs, mod, sample_row,
                              row2(g_post_mix[l]), row2(g_pre_ffn[l]), w_r)
        xs = channel_mixer(res, res[0], sample_row)

    cst = jnp.stack(cst_l, axis=1)
    return (xp.reshape(batch, seq, D_MODEL), xs.reshape(dec_batch, dec_seq, D_MODEL),
            jnp.stack(ka_l, axis=1), jnp.stack(va_l, axis=1),
            jnp.stack(kc_l, axis=1), jnp.stack(vc_l, axis=1),
            cst[..., :HEAD_DIM], cst[..., HEAD_DIM], jnp.stack(m_l, axis=1))
```
